```python
import math
import jax, jax.numpy as jnp
from jax import lax
import numpy as np

D_MODEL = 1024
BATCH = 2
SEQ = 8192
DEPTH = 2

N_MIXERS = 2
N_SB_LAYERS = (DEPTH + N_MIXERS - 1) // N_MIXERS
N_NSA_LAYERS = DEPTH // N_MIXERS
N_HEADS = 16
HEAD_DIM = 64
MIX_WIDTH = N_HEADS * HEAD_DIM
N_KV_HEADS = 4
GROUP = N_HEADS // N_KV_HEADS
SB_IN = 3 * MIX_WIDTH
NSA_IN = MIX_WIDTH + 6 * N_KV_HEADS * HEAD_DIM + 3 * N_HEADS
D_FF = ((8 * D_MODEL + 2) // 3 + 255) // 256 * 256
PLE_DIM = 256
Q_BLOCK = 128
REL_BUCKETS = 32
REL_MAX_DIST = 128
CMP_LEN = 32
CMP_STRIDE = 16
CMP_HIDDEN = 256
SEL_LEN = 64
SEL_TOPK = 16
WINDOW = 512
FORCED_SCORE = 100.0
EPS = 1e-6

kernel_name = "hybrid_stickbreak_nsa_trunk"


def rms_norm(x, g):
    xf = x.astype(jnp.float32)
    y = xf * lax.rsqrt(jnp.mean(xf * xf, axis=-1, keepdims=True) + EPS)
    return (y * g.astype(jnp.float32)).astype(x.dtype)


def rel_bucket(dist):
    n = jnp.maximum(dist, 0)
    exact = REL_BUCKETS // 2
    nf = jnp.maximum(n, 1).astype(jnp.float32)
    large = exact + (jnp.log(nf / exact) / math.log(REL_MAX_DIST / exact)
                     * (REL_BUCKETS - exact)).astype(jnp.int32)
    large = jnp.minimum(large, REL_BUCKETS - 1)
    return jnp.where(n < exact, n, large)


def swiglu(x, w_in, w_out):
    a, b = jnp.split(x @ w_in, 2, axis=-1)
    return (jax.nn.silu(a) * b) @ w_out


def stick_breaking_attention(h, w_in, w_out):
    B, S, _ = h.shape
    qkv = (h @ w_in).reshape(B, S, 3, N_HEADS, HEAD_DIM).transpose(2, 0, 3, 1, 4)
    q, k, v = qkv[0], qkv[1], qkv[2]
    scale = HEAD_DIM ** -0.5
    kpos = jnp.arange(S)

    def block(q0):
        qb = lax.dynamic_slice_in_dim(q, q0, Q_BLOCK, axis=2)
        z = jnp.einsum('bhqd,bhkd->bhqk', qb, k).astype(jnp.float32) * scale
        t = q0 + jnp.arange(Q_BLOCK)
        causal = kpos[None, :] < t[:, None]
        log_fail = jnp.where(causal, jax.nn.log_sigmoid(-z), 0.0)
        after = lax.cumsum(log_fail, axis=3, reverse=True) - log_fail
        a = jnp.where(causal, jnp.exp(jax.nn.log_sigmoid(z) + after), 0.0)
        return jnp.einsum('bhqk,bhkd->bhqd', a, v.astype(jnp.float32))

    out = lax.map(block, jnp.arange(0, S, Q_BLOCK))
    out = out.transpose(1, 0, 3, 2, 4).reshape(B, S, MIX_WIDTH).astype(h.dtype)
    return out @ w_out


def native_sparse_attention(h, w_in, w_out, rel_bias, pe_k, pe_v,
                            ck_w1, ck_w2, cv_w1, cv_w2):
    B, S, _ = h.shape
    scale = HEAD_DIM ** -0.5
    proj = h @ w_in
    q = proj[..., :MIX_WIDTH].reshape(B, S, N_KV_HEADS, GROUP, HEAD_DIM)
    q = q.transpose(0, 2, 3, 1, 4)
    kv_end = MIX_WIDTH + 6 * N_KV_HEADS * HEAD_DIM
    kv = proj[..., MIX_WIDTH:kv_end].reshape(B, S, 6, N_KV_HEADS, HEAD_DIM)
    kv = kv.transpose(2, 0, 3, 1, 4)
    k_cmp, v_cmp, k_sel, v_sel, k_win, v_win = (kv[i] for i in range(6))
    gates = jax.nn.sigmoid(proj[..., kv_end:].astype(jnp.float32))
    gates = gates.reshape(B, S, 3, N_KV_HEADS, GROUP).transpose(2, 0, 3, 4, 1)

    n_cmp = (S - CMP_LEN) // CMP_STRIDE + 1
    win_idx = jnp.arange(n_cmp)[:, None] * CMP_STRIDE + jnp.arange(CMP_LEN)[None, :]

    def compress(kx, pe, w1, w2):
        blocks = kx[:, :, win_idx] + pe
        flat = blocks.reshape(B, N_KV_HEADS, n_cmp, CMP_LEN * HEAD_DIM)
        return jax.nn.gelu(flat @ w1) @ w2

    kc = compress(k_cmp, pe_k, ck_w1, ck_w2)
    vc = compress(v_cmp, pe_v, cv_w1, cv_w2)
    cmp_start = jnp.arange(n_cmp) * CMP_STRIDE
    cmp_end = cmp_start + CMP_LEN - 1

    n_sel = S // SEL_LEN
    top_k = min(SEL_TOPK, n_sel)
    sel_ids = jnp.arange(n_sel)
    sel_start = sel_ids * SEL_LEN
    overlap = ((cmp_start[:, None] < sel_start[None, :] + SEL_LEN)
               & (cmp_start[:, None] + CMP_LEN > sel_start[None, :])).astype(jnp.float32)
    ks_b = k_sel.reshape(B, N_KV_HEADS, n_sel, SEL_LEN, HEAD_DIM)
    vs_b = v_sel.reshape(B, N_KV_HEADS, n_sel, SEL_LEN, HEAD_DIM)
    b_i = jnp.arange(B)[:, None, None, None]
    h_i = jnp.arange(N_KV_HEADS)[None, :, None, None]
    h_i5 = h_i[..., None]
    table_t = rel_bias.reshape(REL_BUCKETS, N_KV_HEADS, GROUP).transpose(1, 0, 2)

    pad = ((0, 0), (0, 0), (WINDOW, 0), (0, 0))
    kw_pad = jnp.pad(k_win, pad)
    vw_pad = jnp.pad(v_win, pad)
    slab = Q_BLOCK + WINDOW
    dist_w = jnp.arange(Q_BLOCK)[:, None] - jnp.arange(slab)[None, :] + WINDOW
    band = (dist_w >= 0) & (dist_w < WINDOW)
    bias_w = rel_bias[rel_bucket(dist_w)].reshape(Q_BLOCK, slab, N_KV_HEADS, GROUP)
    bias_w = bias_w.transpose(2, 3, 0, 1).astype(jnp.float32)

    def block(q0):
        t = q0 + jnp.arange(Q_BLOCK)
        qb = lax.dynamic_slice_in_dim(q, q0, Q_BLOCK, axis=3)

        dist_c = t[:, None] - cmp_end[None, :]
        bias_c = rel_bias[rel_bucket(dist_c)].reshape(Q_BLOCK, n_cmp, N_KV_HEADS, GROUP)
        bias_c = bias_c.transpose(2, 3, 0, 1).astype(jnp.float32)
        s_c = jnp.einsum('bhgqd,bhnd->bhgqn', qb, kc).astype(jnp.float32) * scale + bias_c
        s_c = jnp.where(dist_c >= 0, s_c, -jnp.inf)
        m_c = jnp.max(s_c, axis=-1, keepdims=True)
        e_c = jnp.exp(s_c - jnp.where(jnp.isfinite(m_c), m_c, 0.0))
        den_c = jnp.sum(e_c, axis=-1, keepdims=True)
        p_c = e_c / jnp.where(den_c > 0, den_c, 1.0)
        o_c = jnp.einsum('bhgqn,bhnd->bhgqd', p_c, vc.astype(jnp.float32))

        imp = jnp.einsum('bhgqn,ns->bhqs', p_c, overlap)
        cur = t // SEL_LEN
        forced = ((sel_ids[None, :] == 0) | (sel_ids[None, :] == cur[:, None])
                  | (sel_ids[None, :] == cur[:, None] - 1))
        causal_blk = sel_ids[None, :] <= cur[:, None]
        imp = jnp.where(forced, FORCED_SCORE, jnp.where(causal_blk, imp, -1.0))
        _, idx = lax.top_k(imp, top_k)
        kg = ks_b[b_i, h_i, idx]
        vg = vs_b[b_i, h_i, idx]
        pos = idx[..., None] * SEL_LEN + jnp.arange(SEL_LEN)
        dist_s = t[None, None, :, None, None] - pos
        bias_s = jnp.moveaxis(table_t[h_i5, rel_bucket(dist_s)], -1, 2).astype(jnp.float32)
        s_s = jnp.einsum('bhgqd,bhqkld->bhgqkl', qb, kg).astype(jnp.float32) * scale + bias_s
        s_s = jnp.where((dist_s >= 0)[:, :, None], s_s, -jnp.inf)
        s_s = s_s.reshape(B, N_KV_HEADS, GROUP, Q_BLOCK, top_k * SEL_LEN)
        p_s = jax.nn.softmax(s_s, axis=-1).reshape(B, N_KV_HEADS, GROUP, Q_BLOCK, top_k, SEL_LEN)
        o_s = jnp.einsum('bhgqkl,bhqkld->bhgqd', p_s, vg.astype(jnp.float32))

        kw = lax.dynamic_slice_in_dim(kw_pad, q0, slab, axis=2)
        vw = lax.dynamic_slice_in_dim(vw_pad, q0, slab, axis=2)
        ok_w = band & ((q0 - WINDOW + jnp.arange(slab)) >= 0)[None, :]
        s_w = jnp.einsum('bhgqd,bhkd->bhgqk', qb, kw).astype(jnp.float32) * scale + bias_w
        p_w = jax.nn.softmax(jnp.where(ok_w, s_w, -jnp.inf), axis=-1)
        o_w = jnp.einsum('bhgqk,bhkd->bhgqd', p_w, vw.astype(jnp.float32))

        g = lax.dynamic_slice_in_dim(gates, q0, Q_BLOCK, axis=4)[..., None]
        return g[0] * o_c + g[1] * o_s + g[2] * o_w

    out = lax.map(block, jnp.arange(0, S, Q_BLOCK))
    out = out.transpose(1, 0, 4, 2, 3, 5).reshape(B, S, MIX_WIDTH).astype(h.dtype)
    return out @ w_out


def setup_inputs(seed: int = 0) -> dict:
    key = jax.random.key(seed)
    ks = jax.random.split(key, 24)
    f32 = jnp.float32

    def nrm(k, shape, scale):
        return jax.random.normal(k, shape, f32) * scale

    def gain(k, shape):
        return 1.0 + 0.01 * jax.random.normal(k, shape, f32)

    return {
        "x": nrm(ks[0], (BATCH, SEQ, D_MODEL), 1.0),
        "p": nrm(ks[1], (DEPTH, BATCH, SEQ, PLE_DIM), 1.0),
        "rel_bias": nrm(ks[2], (REL_BUCKETS, N_HEADS), 0.5),
        "norm_mix": gain(ks[3], (DEPTH, D_MODEL)),
        "norm_ffn": gain(ks[4], (DEPTH, D_MODEL)),
        "norm_ple": gain(ks[5], (DEPTH, D_MODEL)),
        "final_norm": gain(ks[6], (D_MODEL,)),
        "sb_w_in": nrm(ks[7], (N_SB_LAYERS, D_MODEL, SB_IN), D_MODEL ** -0.5),
        "sb_w_out": nrm(ks[8], (N_SB_LAYERS, MIX_WIDTH, D_MODEL), MIX_WIDTH ** -0.5),
        "nsa_w_in": nrm(ks[9], (N_NSA_LAYERS, D_MODEL, NSA_IN), D_MODEL ** -0.5),
        "nsa_w_out": nrm(ks[10], (N_NSA_LAYERS, MIX_WIDTH, D_MODEL), MIX_WIDTH ** -0.5),
        "nsa_pe_k": nrm(ks[11], (N_NSA_LAYERS, CMP_LEN, HEAD_DIM), 0.02),
        "nsa_pe_v": nrm(ks[12], (N_NSA_LAYERS, CMP_LEN, HEAD_DIM), 0.02),
        "nsa_ck_w1": nrm(ks[13], (N_NSA_LAYERS, CMP_LEN * HEAD_DIM, CMP_HIDDEN), (CMP_LEN * HEAD_DIM) ** -0.5),
        "nsa_ck_w2": nrm(ks[14], (N_NSA_LAYERS, CMP_HIDDEN, HEAD_DIM), CMP_HIDDEN ** -0.5),
        "nsa_cv_w1": nrm(ks[15], (N_NSA_LAYERS, CMP_LEN * HEAD_DIM, CMP_HIDDEN), (CMP_LEN * HEAD_DIM) ** -0.5),
        "nsa_cv_w2": nrm(ks[16], (N_NSA_LAYERS, CMP_HIDDEN, HEAD_DIM), CMP_HIDDEN ** -0.5),
        "ffn_w_in": nrm(ks[17], (DEPTH, D_MODEL, 2 * D_FF), D_MODEL ** -0.5),
        "ffn_w_out": nrm(ks[18], (DEPTH, D_FF, D_MODEL), D_FF ** -0.5),
        "ple_w_proj": nrm(ks[19], (DEPTH, PLE_DIM, D_MODEL), PLE_DIM ** -0.5),
        "ple_w_gate": nrm(ks[20], (DEPTH, D_MODEL, D_MODEL), D_MODEL ** -0.5),
    }


def reference(x, p, rel_bias, norm_mix, norm_ffn, norm_ple, final_norm,
              sb_w_in, sb_w_out, nsa_w_in, nsa_w_out, nsa_pe_k, nsa_pe_v,
              nsa_ck_w1, nsa_ck_w2, nsa_cv_w1, nsa_cv_w2,
              ffn_w_in, ffn_w_out, ple_w_proj, ple_w_gate):
    h = x
    for i in range(DEPTH):
        j = i // N_MIXERS
        hn = rms_norm(h, norm_mix[i])
        if i % N_MIXERS == 0:
            mix = stick_breaking_attention(hn, sb_w_in[j], sb_w_out[j])
        else:
            mix = native_sparse_attention(hn, nsa_w_in[j], nsa_w_out[j], rel_bias,
                                          nsa_pe_k[j], nsa_pe_v[j], nsa_ck_w1[j],
                                          nsa_ck_w2[j], nsa_cv_w1[j], nsa_cv_w2[j])
        h = h + mix
        h = h + swiglu(rms_norm(h, norm_ffn[i]), ffn_w_in[i], ffn_w_out[i])
        gate = jax.nn.sigmoid(rms_norm(h, norm_ple[i]) @ ple_w_gate[i])
        h = h + gate * (p[i] @ ple_w_proj[i])
    return rms_norm(h, final_norm)
```

```python
import functools
import math

import numpy as np
import jax
import jax.numpy as jnp
from jax import lax
from jax.experimental import pallas as pl
from jax.experimental.pallas import tpu as pltpu

F32 = jnp.float32
MXU_DTYPE = jnp.bfloat16

N_HEADS = 16
HEAD_DIM = 64
N_KV_HEADS = 4
GROUP = N_HEADS // N_KV_HEADS
REL_BUCKETS = 32
REL_MAX_DIST = 128
CMP_LEN = 32
CMP_STRIDE = 16
SEL_LEN = 64
SEL_TOPK = 16
WINDOW = 512
FORCED_SCORE = 100.0
EPS = 1e-6
NEG = -1e9

LANES = 128
KEY_TILE = 256
CMP_PAD = 16
VMEM_LIMIT = 56 * 1024 * 1024


def _dot(a, b):
    return jnp.dot(a, b, preferred_element_type=F32)


def _rms(x, g):
    ms = jnp.mean(x * x, axis=-1, keepdims=True)
    return x * lax.rsqrt(ms + EPS) * g


def _const_spec(shape):
    nd = len(shape)
    return pl.BlockSpec(shape, lambda *_: (0,) * nd, pipeline_mode=pl.Buffered(1))


def _params(n_axes):
    return pltpu.CompilerParams(dimension_semantics=("arbitrary",) * n_axes,
                                vmem_limit_bytes=VMEM_LIMIT)


def _norm_matmul_body(x_ref, g_ref, w_ref, o_ref, *, col_chunk):
    xn = _rms(x_ref[...], g_ref[...]).astype(MXU_DTYPE)
    for c in range(0, o_ref.shape[-1], col_chunk):
        o_ref[:, c:c + col_chunk] = _dot(xn, w_ref[:, c:c + col_chunk]).astype(o_ref.dtype)


def _norm_matmul(x, g, w, out_dtype, *, tm=512, col_chunk=512):
    t, d = x.shape
    n = w.shape[1]
    assert t % tm == 0 and n % col_chunk == 0
    return pl.pallas_call(
        functools.partial(_norm_matmul_body, col_chunk=col_chunk),
        grid=(t // tm,),
        in_specs=[pl.BlockSpec((tm, d), lambda i: (i, 0)),
                  _const_spec((1, d)),
                  _const_spec((d, n))],
        out_specs=pl.BlockSpec((tm, n), lambda i: (i, 0)),
        out_shape=jax.ShapeDtypeStruct((t, n), out_dtype),
        compiler_params=_params(1),
        name="norm_matmul",
    )(x, g.reshape(1, d), w)


def _post_mixer_body(h_ref, a_ref, wo_ref, gf_ref, win_ref, wout_ref, gp_ref, wg_ref, p_ref, wp_ref,
                     gfin_ref, o_ref, *, d_ff, chunk, final):
    h1 = h_ref[...] + _dot(a_ref[...], wo_ref[...])
    xn = _rms(h1, gf_ref[...]).astype(MXU_DTYPE)
    acc = jnp.zeros_like(h1)
    for c in range(0, d_ff, chunk):
        a = _dot(xn, win_ref[:, c:c + chunk])
        b = _dot(xn, win_ref[:, d_ff + c:d_ff + c + chunk])
        act = a * jax.nn.sigmoid(a) * b
        acc = acc + _dot(act.astype(MXU_DTYPE), wout_ref[c:c + chunk, :])
    h2 = h1 + acc
    xg = _rms(h2, gp_ref[...]).astype(MXU_DTYPE)
    gate = jax.nn.sigmoid(_dot(xg, wg_ref[...]))
    h3 = h2 + gate * _dot(p_ref[...].astype(MXU_DTYPE), wp_ref[...])
    if final:
        h3 = _rms(h3, gfin_ref[...])
    o_ref[...] = h3


def _post_mixer(h, attn, wo, g_ffn, w_in, w_out, g_ple, wg, p, wp, g_final, *, final, tm=512, chunk=256):
    t, d = h.shape
    d_ff = w_out.shape[0]
    ple = p.shape[1]
    assert t % tm == 0 and d_ff % chunk == 0
    row = lambda i: (i, 0)
    return pl.pallas_call(
        functools.partial(_post_mixer_body, d_ff=d_ff, chunk=chunk, final=final),
        grid=(t // tm,),
        in_specs=[pl.BlockSpec((tm, d), row),
                  pl.BlockSpec((tm, d), row),
                  _const_spec((d, d)),
                  _const_spec((1, d)),
                  _const_spec((d, 2 * d_ff)),
                  _const_spec((d_ff, d)),
                  _const_spec((1, d)),
                  _const_spec((d, d)),
                  pl.BlockSpec((tm, ple), row),
                  _const_spec((ple, d)),
                  _const_spec((1, d))],
        out_specs=pl.BlockSpec((tm, d), row),
        out_shape=jax.ShapeDtypeStruct((t, d), F32),
        compiler_params=_params(1),
        name="post_mixer",
    )(h, attn, wo, g_ffn.reshape(1, d), w_in, w_out, g_ple.reshape(1, d), wg, p, wp,
      g_final.reshape(1, d))


def _suffix_matrix():
    j = np.arange(2 * LANES)[:, None] % LANES
    c = np.arange(2 * LANES)[None, :]
    return ((c >= LANES) | (j > c)).astype(np.float32)


def _sb_tile(q, kt_tile, v_tile, mm, carry, causal):
    z = _dot(q, kt_tile)
    sp = jnp.maximum(z, 0.0) + jnp.log(1.0 + jnp.exp(-jnp.abs(z)))
    if causal is not None:
        sp = jnp.where(causal, sp, 0.0)
    hi = sp.astype(MXU_DTYPE)
    lo = (sp - hi.astype(F32)).astype(MXU_DTYPE)
    cums = []
    for half in (1, 0):
        sl = slice(half * LANES, (half + 1) * LANES)
        r = _dot(jnp.concatenate([hi[:, sl], lo[:, sl]], axis=1), mm)
        cums.append(carry + r[:, :LANES])
        carry = carry + r[:, LANES:]
    cum = jnp.concatenate([cums[1], cums[0]], axis=1)
    a = jnp.exp(z - sp - cum)
    if causal is not None:
        a = jnp.where(causal, a, 0.0)
    return _dot(a.astype(MXU_DTYPE), v_tile), carry


def _sb_body(q_ref, kt_ref, v_ref, mm_ref, o_ref, acc_ref, carry_ref, *, tq):
    q0 = pl.program_id(2) * tq
    n_off = q0 // KEY_TILE
    mm = mm_ref[...]
    acc_ref[...] = jnp.zeros_like(acc_ref)
    for h in range(2):
        q = (q_ref[h] * (HEAD_DIM ** -0.5)).astype(MXU_DTYPE)
        carry_ref[...] = jnp.zeros_like(carry_ref)
        for d in reversed(range(tq // KEY_TILE)):
            r0 = d * KEY_TILE
            k0 = pl.multiple_of(q0 + r0, KEY_TILE)
            rows = lax.broadcasted_iota(jnp.int32, (tq - r0, KEY_TILE), 0)
            cols = lax.broadcasted_iota(jnp.int32, (tq - r0, KEY_TILE), 1)
            pv, carry = _sb_tile(q[r0:], kt_ref[h, :, pl.ds(k0, KEY_TILE)],
                                 v_ref[h, pl.ds(k0, KEY_TILE), :], mm, carry_ref[r0:, :], cols < rows)
            carry_ref[r0:, :] = carry
            acc_ref[r0:, :] += pv

        def off_diag(it, _):
            k0 = pl.multiple_of((n_off - 1 - it) * KEY_TILE, KEY_TILE)
            pv, carry = _sb_tile(q, kt_ref[h, :, pl.ds(k0, KEY_TILE)],
                                 v_ref[h, pl.ds(k0, KEY_TILE), :], mm, carry_ref[...], None)
            carry_ref[...] = carry
            acc_ref[...] += pv
            return 0

        lax.fori_loop(0, n_off, off_diag, 0)
    o_ref[...] = acc_ref[...].astype(o_ref.dtype)


def _sb_attention(q, kt, vpad, *, tq=512):
    b, nh, s, _ = q.shape
    assert s % tq == 0 and tq % KEY_TILE == 0
    mm = jnp.asarray(_suffix_matrix(), MXU_DTYPE)
    return pl.pallas_call(
        functools.partial(_sb_body, tq=tq),
        grid=(b, nh // 2, s // tq),
        in_specs=[pl.BlockSpec((None, 2, tq, HEAD_DIM), lambda bi, hp, i: (bi, hp, i, 0)),
                  pl.BlockSpec((None, 2, HEAD_DIM, s), lambda bi, hp, i: (bi, hp, 0, 0)),
                  pl.BlockSpec((None, 2, s, LANES), lambda bi, hp, i: (bi, hp, 0, 0)),
                  _const_spec((2 * LANES, 2 * LANES))],
        out_specs=pl.BlockSpec((None, tq, LANES), lambda bi, hp, i: (bi, i, hp)),
        out_shape=jax.ShapeDtypeStruct((b, s, nh * HEAD_DIM), MXU_DTYPE),
        scratch_shapes=[pltpu.VMEM((tq, LANES), F32), pltpu.VMEM((tq, LANES), F32)],
        compiler_params=_params(3),
        name="stick_breaking",
    )(q, kt, vpad, mm)


def _compress_body(c_ref, pe_ref, w1_ref, w2_ref, o_ref):
    c = c_ref[...]
    nch, half = c.shape
    top = _dot((c + pe_ref[0:1, :]).astype(MXU_DTYPE), w1_ref[:half, :])
    bot = _dot((c + pe_ref[1:2, :]).astype(MXU_DTYPE), w1_ref[half:, :])
    hid = top + pltpu.roll(bot, nch - 1, 0)
    o_ref[...] = _dot(jax.nn.gelu(hid).astype(MXU_DTYPE), w2_ref[...])


def _compress(chunks, pe, w1, w2):
    _, b, hk, nch, half = chunks.shape
    hid = w1.shape[-1]
    return pl.pallas_call(
        _compress_body,
        grid=(2, b, hk),
        in_specs=[pl.BlockSpec((None, None, None, nch, half), lambda c, bi, h: (c, bi, h, 0, 0)),
                  pl.BlockSpec((None, 2, half), lambda c, bi, h: (c, 0, 0)),
                  pl.BlockSpec((None, 2 * half, hid), lambda c, bi, h: (c, 0, 0)),
                  pl.BlockSpec((None, hid, HEAD_DIM), lambda c, bi, h: (c, 0, 0))],
        out_specs=pl.BlockSpec((None, None, None, nch, HEAD_DIM), lambda c, bi, h: (c, bi, h, 0, 0)),
        out_shape=jax.ShapeDtypeStruct((2, b, hk, nch, HEAD_DIM), F32),
        compiler_params=_params(3),
        name="nsa_compress",
    )(chunks, pe, w1, w2)


def _nsa_body(qs_ref, qt_ref, kc_ref, vct_ref, ovt_ref, bct_ref, kst_ref, vs_ref, bs_ref,
              kwt_ref, vw_ref, bw_ref, g_ref, o_ref, sc_ref, *, tq):
    i = pl.program_id(2)
    q0 = i * tq
    rows = GROUP * tq
    ncp = kc_ref.shape[0]
    nsel = ovt_ref.shape[0]
    band = bct_ref.shape[0]

    qt = jnp.concatenate([qt_ref[g] for g in range(GROUP)], axis=1)
    sc_ref[...] = _dot(kc_ref[...], qt)
    j0 = pl.multiple_of(i * (tq // CMP_STRIDE), CMP_STRIDE)
    sc_ref[pl.ds(j0, band), :] += bct_ref[...]
    jp = lax.broadcasted_iota(jnp.int32, (ncp, rows), 0)
    tcol = q0 + (lax.broadcasted_iota(jnp.int32, (ncp, rows), 1) & (tq - 1))
    valid = (jp >= CMP_PAD) & (CMP_STRIDE * (jp - CMP_PAD) + (CMP_LEN - 1) <= tcol)
    s = jnp.where(valid, sc_ref[...], -jnp.inf)
    m = jnp.max(s, axis=0, keepdims=True)
    e = jnp.exp(s - jnp.where(m > -jnp.inf, m, 0.0))
    den = jnp.sum(e, axis=0, keepdims=True)
    pc = (e / jnp.where(den > 0, den, 1.0)).astype(MXU_DTYPE)
    o_c = _dot(vct_ref[...], pc).T
    imp = _dot(ovt_ref[...], pc[:, :tq])
    for g in range(1, GROUP):
        imp = imp + _dot(ovt_ref[...], pc[:, g * tq:(g + 1) * tq])

    sid = lax.broadcasted_iota(jnp.int32, (nsel, tq), 0)
    cur = (q0 + lax.broadcasted_iota(jnp.int32, (nsel, tq), 1)) // SEL_LEN
    forced = (sid == 0) | (sid == cur) | (sid == cur - 1)
    v = jnp.where(forced, FORCED_SCORE, jnp.where(sid <= cur, imp, -1.0))
    mask_t = jnp.full((nsel, tq), NEG, F32)
    for _ in range(min(SEL_TOPK, nsel)):
        mx = jnp.max(v, axis=0, keepdims=True)
        idx = jnp.min(jnp.where(v == mx, sid, nsel), axis=0, keepdims=True)
        hit = sid == idx
        mask_t = jnp.where(hit, 0.0, mask_t)
        v = jnp.where(hit, -jnp.inf, v)
    mask = mask_t.T.astype(MXU_DTYPE)
    qs = qs_ref[...].reshape(rows, HEAD_DIM)
    qa = jnp.concatenate([jnp.concatenate([mask] * GROUP, axis=0), qs], axis=1)

    def far_tile(kt, carry):
        m_run, acc = carry
        k0 = pl.multiple_of(WINDOW + kt * KEY_TILE, KEY_TILE)
        s_f = _dot(qa, kst_ref[:, pl.ds(k0, KEY_TILE)])
        m_new = jnp.maximum(m_run, jnp.max(s_f, axis=1, keepdims=True))
        p_f = jnp.exp(s_f - m_new).astype(MXU_DTYPE)
        acc = jnp.exp(m_run - m_new) * acc + _dot(p_f, vs_ref[pl.ds(k0, KEY_TILE), :])
        return m_new, acc

    near = bs_ref.shape[1]
    n_far = jnp.maximum(q0 - (near - tq), 0) // KEY_TILE
    m_run, acc = lax.fori_loop(0, n_far, far_tile,
                               (jnp.full((rows, 1), -jnp.inf, F32), jnp.zeros((rows, LANES), F32)))
    k0 = pl.multiple_of(q0 + WINDOW - (near - tq), KEY_TILE)
    s_n = _dot(qa, kst_ref[:, pl.ds(k0, near)]) + bs_ref[...]
    col = lax.broadcasted_iota(jnp.int32, (rows, near), 1)
    s_n = jnp.where(q0 - (near - tq) + col >= 0, s_n, NEG)
    m_new = jnp.maximum(m_run, jnp.max(s_n, axis=1, keepdims=True))
    p_n = jnp.exp(s_n - m_new).astype(MXU_DTYPE)
    acc = jnp.exp(m_run - m_new) * acc + _dot(p_n, vs_ref[pl.ds(k0, near), :])
    o_s = acc[:, :HEAD_DIM] / acc[:, HEAD_DIM:HEAD_DIM + 1]

    slab = bw_ref.shape[1]
    kw0 = pl.multiple_of(q0, KEY_TILE)
    s_w = _dot(qs, kwt_ref[:, pl.ds(kw0, slab)]) + bw_ref[...]
    col = lax.broadcasted_iota(jnp.int32, (rows, slab), 1)
    s_w = jnp.where(q0 - WINDOW + col >= 0, s_w, NEG)
    p_w = jnp.exp(s_w - jnp.max(s_w, axis=1, keepdims=True)).astype(MXU_DTYPE)
    acc_w = _dot(p_w, vw_ref[pl.ds(kw0, slab), :])
    o_w = acc_w[:, :HEAD_DIM] / acc_w[:, HEAD_DIM:HEAD_DIM + 1]

    sg = jax.nn.sigmoid(g_ref[...])

    def gate(branch):
        return jnp.concatenate([sg[:, branch * GROUP + g:branch * GROUP + g + 1] for g in range(GROUP)], axis=0)

    out = gate(0) * o_c + gate(1) * o_s + gate(2) * o_w
    o_ref[...] = out.reshape(GROUP, tq, HEAD_DIM).astype(o_ref.dtype)


def _rel_bucket_table(n):
    d = np.arange(n)
    exact = REL_BUCKETS // 2
    nf = np.maximum(d, 1).astype(np.float32)
    large = exact + (np.log(nf / np.float32(exact)) / np.float32(math.log(REL_MAX_DIST / exact))
                     * np.float32(REL_BUCKETS - exact)).astype(np.int32)
    return np.where(d < exact, d, np.minimum(large, REL_BUCKETS - 1)).astype(np.int32)


def _bias_tile(rel_bias, dist, valid, shifted, invalid_value):
    bucket = _rel_bucket_table(int(dist.max()) + 1)[np.maximum(dist, 0)]
    tile = rel_bias[bucket]
    if shifted:
        tile = tile - rel_bias[REL_BUCKETS - 1]
    tile = jnp.where(valid[..., None], tile, invalid_value)
    tile = jnp.moveaxis(tile, -1, 0)
    return tile.reshape((N_KV_HEADS, GROUP) + dist.shape)


def _nsa_attention(qs, qt, kc, vct, kst, vs, kwt, vw, gates, rel_bias, *, tq=256):
    b, nh, s, _ = qs.shape
    ncp = kc.shape[2]
    nsel = s // SEL_LEN
    sp = kst.shape[-1]
    rows = GROUP * tq
    near = 2 * tq
    slab = tq + WINDOW
    band = tq // CMP_STRIDE + 16
    assert s % tq == 0 and tq % KEY_TILE == 0 and tq & (tq - 1) == 0 and sp == s + WINDOW

    jc = np.arange(ncp)[None, :] - CMP_PAD
    cs = jc * CMP_STRIDE
    ss = np.arange(nsel)[:, None] * SEL_LEN
    ovt = ((jc >= 0) & (jc < (s - CMP_LEN) // CMP_STRIDE + 1) & (cs < ss + SEL_LEN) & (cs + CMP_LEN > ss))
    ovt = jnp.asarray(ovt.astype(np.float32), MXU_DTYPE)

    r = np.arange(tq)
    d_c = r[None, :] + (16 * CMP_STRIDE - (CMP_LEN - 1)) - CMP_STRIDE * np.arange(band)[:, None]
    bct = _bias_tile(rel_bias, d_c, d_c >= 0, True, 0.0)
    bct = bct.transpose(0, 2, 1, 3).reshape(N_KV_HEADS, band, rows)
    d_s = r[:, None] - np.arange(near)[None, :] + (near - tq)
    bs = _bias_tile(rel_bias, d_s, d_s >= 0, True, NEG).reshape(N_KV_HEADS, rows, near)
    d_w = r[:, None] - np.arange(slab)[None, :] + WINDOW
    bw = _bias_tile(rel_bias, d_w, (d_w >= 0) & (d_w < WINDOW), False, NEG).reshape(N_KV_HEADS, rows, slab)

    per_head = lambda *shape: pl.BlockSpec((None, None) + shape, lambda bi, h, i: (bi, h, 0, 0))
    per_kvh = lambda *shape: pl.BlockSpec((None,) + shape, lambda bi, h, i: (h, 0, 0))
    return pl.pallas_call(
        functools.partial(_nsa_body, tq=tq),
        grid=(b, N_KV_HEADS, s // tq),
        in_specs=[pl.BlockSpec((None, GROUP, tq, HEAD_DIM), lambda bi, h, i: (bi, h, i, 0)),
                  pl.BlockSpec((None, GROUP, HEAD_DIM, tq), lambda bi, h, i: (bi, h, 0, i)),
                  per_head(ncp, HEAD_DIM),
                  per_head(HEAD_DIM, ncp),
                  _const_spec((nsel, ncp)),
                  per_kvh(band, rows),
                  per_head(nsel + HEAD_DIM, sp),
                  per_head(sp, LANES),
                  per_kvh(rows, near),
                  per_head(HEAD_DIM, sp),
                  per_head(sp, LANES),
                  per_kvh(rows, slab),
                  pl.BlockSpec((None, None, tq, 16), lambda bi, h, i: (bi, h, i, 0))],
        out_specs=pl.BlockSpec((None, GROUP, tq, HEAD_DIM), lambda bi, h, i: (bi, h, i, 0)),
        out_shape=jax.ShapeDtypeStruct((b, nh, s, HEAD_DIM), MXU_DTYPE),
        scratch_shapes=[pltpu.VMEM((ncp, rows), F32)],
        compiler_params=_params(3),
        name="nsa_attention",
    )(qs, qt, kc, vct, ovt, bct, kst, vs, bs, kwt, vw, bw, gates)


def _with_ones(v):
    ones = jnp.ones(v.shape[:-1] + (1,), v.dtype)
    zeros = jnp.zeros(v.shape[:-1] + (LANES - HEAD_DIM - 1,), v.dtype)
    return jnp.concatenate([v, ones, zeros], axis=-1)


def _pad_keys(x, axis):
    pad = [(0, 0)] * x.ndim
    pad[axis] = (WINDOW, 0)
    return jnp.pad(x, pad)


def _nsa_mixer(hn, w_in, rel_bias, pe_k, pe_v, ck_w1, ck_w2, cv_w1, cv_w2, norm_g, b, s):
    d = hn.shape[-1]
    mix = N_HEADS * HEAD_DIM
    kv_end = mix + 6 * N_KV_HEADS * HEAD_DIM
    n_in = w_in.shape[1]
    n_pad = -n_in % KEY_TILE
    w = jnp.pad(w_in, ((0, 0), (0, n_pad))).astype(MXU_DTYPE)
    proj = _norm_matmul(hn, norm_g, w, F32, col_chunk=KEY_TILE).reshape(b, s, n_in + n_pad)

    q = proj[..., :mix].reshape(b, s, N_HEADS, HEAD_DIM).transpose(0, 2, 1, 3)
    qs = (q * (HEAD_DIM ** -0.5)).astype(MXU_DTYPE)
    qt = qs.transpose(0, 1, 3, 2)
    kv = proj[..., mix:kv_end].reshape(b, s, 6, N_KV_HEADS, HEAD_DIM).transpose(2, 0, 3, 1, 4)
    gates = proj[..., kv_end:kv_end + 3 * N_HEADS].reshape(b, s, 3, N_KV_HEADS, GROUP)
    gates = gates.transpose(0, 3, 1, 2, 4).reshape(b, N_KV_HEADS, s, 3 * GROUP)
    gates = jnp.pad(gates, ((0, 0), (0, 0), (0, 0), (0, 16 - 3 * GROUP)))

    nch = s // CMP_STRIDE
    half = CMP_STRIDE * HEAD_DIM
    chunks = kv[0:2].reshape(2, b, N_KV_HEADS, nch, half)
    pe = jnp.stack([pe_k, pe_v]).reshape(2, 2, half)
    w1 = jnp.stack([ck_w1, cv_w1]).astype(MXU_DTYPE)
    w2 = jnp.stack([ck_w2, cv_w2]).astype(MXU_DTYPE)
    cmp = _compress(chunks, pe, w1, w2)
    ncp = -(-(nch + CMP_PAD) // LANES) * LANES
    cmp = jnp.pad(cmp, ((0, 0), (0, 0), (0, 0), (CMP_PAD, ncp - nch - CMP_PAD), (0, 0))).astype(MXU_DTYPE)
    kc = cmp[0]
    vct = cmp[1].transpose(0, 1, 3, 2)

    nsel = s // SEL_LEN
    onehot = (np.arange(s)[None, :] // SEL_LEN == np.arange(nsel)[:, None]).astype(np.float32)
    onehot = jnp.broadcast_to(jnp.asarray(onehot, MXU_DTYPE), (b, N_KV_HEADS, nsel, s))
    kst = jnp.concatenate([onehot, kv[2].astype(MXU_DTYPE).transpose(0, 1, 3, 2)], axis=2)
    kst = _pad_keys(kst, 3)
    vs = _pad_keys(_with_ones(kv[3].astype(MXU_DTYPE)), 2)
    kwt = _pad_keys(kv[4].astype(MXU_DTYPE).transpose(0, 1, 3, 2), 3)
    vw = _pad_keys(_with_ones(kv[5].astype(MXU_DTYPE)), 2)

    out = _nsa_attention(qs, qt, kc, vct, kst, vs, kwt, vw, gates, rel_bias)
    return out.transpose(0, 2, 1, 3).reshape(b * s, mix)


def _sb_mixer(hn, w_in, norm_g, b, s):
    mix = N_HEADS * HEAD_DIM
    qkv = _norm_matmul(hn, norm_g, w_in.astype(MXU_DTYPE), MXU_DTYPE)
    qkv = qkv.reshape(b, s, 3, N_HEADS, HEAD_DIM).transpose(2, 0, 3, 1, 4)
    q = qkv[0]
    kt = qkv[1].transpose(0, 1, 3, 2)
    v = qkv[2].reshape(b, N_HEADS // 2, 2, s, HEAD_DIM)
    zeros = jnp.zeros_like(v[:, :, 0])
    vpad = jnp.stack([jnp.concatenate([v[:, :, 0], zeros], axis=-1),
                      jnp.concatenate([zeros, v[:, :, 1]], axis=-1)], axis=2)
    vpad = vpad.reshape(b, N_HEADS, s, 2 * HEAD_DIM)
    return _sb_attention(q, kt, vpad).reshape(b * s, mix)


def kernel(x, p, rel_bias, norm_mix, norm_ffn, norm_ple, final_norm, sb_w_in, sb_w_out, nsa_w_in, nsa_w_out,
           nsa_pe_k, nsa_pe_v, nsa_ck_w1, nsa_ck_w2, nsa_cv_w1, nsa_cv_w2, ffn_w_in, ffn_w_out,
           ple_w_proj, ple_w_gate):
    b, s, d = x.shape
    depth = p.shape[0]
    h = x.reshape(b * s, d)
    for i in range(depth):
        j = i // 2
        if i % 2 == 0:
            attn = _sb_mixer(h, sb_w_in[j], norm_mix[i], b, s)
            wo = sb_w_out[j]
        else:
            attn = _nsa_mixer(h, nsa_w_in[j], rel_bias, nsa_pe_k[j], nsa_pe_v[j], nsa_ck_w1[j], nsa_ck_w2[j],
                              nsa_cv_w1[j], nsa_cv_w2[j], norm_mix[i], b, s)
            wo = nsa_w_out[j]
        h = _post_mixer(h, attn, wo.astype(MXU_DTYPE), norm_ffn[i], ffn_w_in[i].astype(MXU_DTYPE),
                        ffn_w_out[i].astype(MXU_DTYPE), norm_ple[i], ple_w_gate[i].astype(MXU_DTYPE),
                        p[i].reshape(b * s, -1), ple_w_proj[i].astype(MXU_DTYPE), final_norm,
                        final=(i == depth - 1))
    return h.reshape(b, s, d)
```

```python
import functools
import math

import numpy as np
import jax
import jax.numpy as jnp
from jax import lax
from jax.experimental import pallas as pl
from jax.experimental.pallas import tpu as pltpu

F32 = jnp.float32
MXU_DTYPE = jnp.bfloat16

N_HEADS = 16
HEAD_DIM = 64
N_KV_HEADS = 4
GROUP = N_HEADS // N_KV_HEADS
REL_BUCKETS = 32
REL_MAX_DIST = 128
CMP_LEN = 32
CMP_STRIDE = 16
SEL_LEN = 64
SEL_TOPK = 16
WINDOW = 512
FORCED_SCORE = 100.0
EPS = 1e-6
NEG = -1e9
LOG2E = 1.4426950408889634
MAX_EXP2 = 126.0

LANES = 128
KEY_TILE = 256
CMP_PAD = 16
VMEM_LIMIT = 56 * 1024 * 1024


def _dot(a, b):
    return jnp.dot(a, b, preferred_element_type=F32)


def _rms(x, g):
    ms = jnp.mean(x * x, axis=-1, keepdims=True)
    return x * lax.rsqrt(ms + EPS) * g


def _const_spec(shape):
    nd = len(shape)
    return pl.BlockSpec(shape, lambda *_: (0,) * nd, pipeline_mode=pl.Buffered(1))


def _params(n_axes):
    return pltpu.CompilerParams(dimension_semantics=("arbitrary",) * n_axes,
                                vmem_limit_bytes=VMEM_LIMIT)


def _norm_matmul_body(x_ref, g_ref, w_ref, o_ref, *, col_chunk):
    xn = _rms(x_ref[...], g_ref[...]).astype(MXU_DTYPE)
    for c in range(0, o_ref.shape[-1], col_chunk):
        o_ref[:, c:c + col_chunk] = _dot(xn, w_ref[:, c:c + col_chunk]).astype(o_ref.dtype)


def _norm_matmul(x, g, w, out_dtype, *, tm=512, col_chunk=512):
    t, d = x.shape
    n = w.shape[1]
    assert t % tm == 0 and n % col_chunk == 0
    return pl.pallas_call(
        functools.partial(_norm_matmul_body, col_chunk=col_chunk),
        grid=(t // tm,),
        in_specs=[pl.BlockSpec((tm, d), lambda i: (i, 0)),
                  _const_spec((1, d)),
                  _const_spec((d, n))],
        out_specs=pl.BlockSpec((tm, n), lambda i: (i, 0)),
        out_shape=jax.ShapeDtypeStruct((t, n), out_dtype),
        compiler_params=_params(1),
        name="norm_matmul",
    )(x, g.reshape(1, d), w)


def _post_mixer_body(h_ref, a_ref, wo_ref, gf_ref, win_ref, wout_ref, gp_ref, wg_ref, p_ref, wp_ref,
                     gfin_ref, o_ref, *, d_ff, chunk, final):
    h1 = h_ref[...] + _dot(a_ref[...], wo_ref[...])
    xn = _rms(h1, gf_ref[...]).astype(MXU_DTYPE)
    acc = jnp.zeros_like(h1)
    for c in range(0, d_ff, chunk):
        a = _dot(xn, win_ref[:, c:c + chunk])
        b = _dot(xn, win_ref[:, d_ff + c:d_ff + c + chunk])
        act = a * jax.nn.sigmoid(a) * b
        acc = acc + _dot(act.astype(MXU_DTYPE), wout_ref[c:c + chunk, :])
    h2 = h1 + acc
    xg = _rms(h2, gp_ref[...]).astype(MXU_DTYPE)
    gate = jax.nn.sigmoid(_dot(xg, wg_ref[...]))
    h3 = h2 + gate * _dot(p_ref[...].astype(MXU_DTYPE), wp_ref[...])
    if final:
        h3 = _rms(h3, gfin_ref[...])
    o_ref[...] = h3


def _post_mixer(h, attn, wo, g_ffn, w_in, w_out, g_ple, wg, p, wp, g_final, *, final, tm=512, chunk=256):
    t, d = h.shape
    d_ff = w_out.shape[0]
    ple = p.shape[1]
    assert t % tm == 0 and d_ff % chunk == 0
    row = lambda i: (i, 0)
    return pl.pallas_call(
        functools.partial(_post_mixer_body, d_ff=d_ff, chunk=chunk, final=final),
        grid=(t // tm,),
        in_specs=[pl.BlockSpec((tm, d), row),
                  pl.BlockSpec((tm, d), row),
                  _const_spec((d, d)),
                  _const_spec((1, d)),
                  _const_spec((d, 2 * d_ff)),
                  _const_spec((d_ff, d)),
                  _const_spec((1, d)),
                  _const_spec((d, d)),
                  pl.BlockSpec((tm, ple), row),
                  _const_spec((ple, d)),
                  _const_spec((1, d))],
        out_specs=pl.BlockSpec((tm, d), row),
        out_shape=jax.ShapeDtypeStruct((t, d), F32),
        compiler_params=_params(1),
        name="post_mixer",
    )(h, attn, wo, g_ffn.reshape(1, d), w_in, w_out, g_ple.reshape(1, d), wg, p, wp,
      g_final.reshape(1, d))


def _suffix_matrix():
    j = np.arange(2 * LANES)[:, None] % LANES
    c = np.arange(2 * LANES)[None, :]
    return ((c >= LANES) | (j > c)).astype(np.float32)


def _sb_tile(q, kt_tile, v_tile, mm, carry, causal):
    z = _dot(q, kt_tile) * LOG2E
    sp = jnp.maximum(z, jnp.log2(1.0 + jnp.exp2(jnp.minimum(z, MAX_EXP2))))
    if causal is not None:
        sp = jnp.where(causal, sp, 0.0)
    hi = sp.astype(MXU_DTYPE)
    lo = (sp - hi.astype(F32)).astype(MXU_DTYPE)
    cums = []
    for half in (1, 0):
        sl = slice(half * LANES, (half + 1) * LANES)
        r = _dot(jnp.concatenate([hi[:, sl], lo[:, sl]], axis=1), mm)
        cums.append(carry + r[:, :LANES])
        carry = carry + r[:, LANES:]
    cum = jnp.concatenate([cums[1], cums[0]], axis=1)
    a = jnp.exp2(z - sp - cum)
    if causal is not None:
        a = jnp.where(causal, a, 0.0)
    return _dot(a.astype(MXU_DTYPE), v_tile), carry


def _sb_body(q_ref, kt_ref, v_ref, mm_ref, o_ref, acc_ref, carry_ref, *, tq):
    q0 = pl.program_id(2) * tq
    n_off = q0 // KEY_TILE
    mm = mm_ref[...]
    first = lax.broadcasted_iota(jnp.int32, (1, LANES), 1) < HEAD_DIM
    lanes_of = (first, jnp.logical_not(first))
    q2 = q_ref[...] * (HEAD_DIM ** -0.5)
    qs = [jnp.where(m, q2, 0).astype(MXU_DTYPE) for m in lanes_of]
    acc_ref[...] = jnp.zeros_like(acc_ref)
    carry_ref[...] = jnp.zeros_like(carry_ref)

    def tile(k0, r0, causal):
        kt_tile = kt_ref[:, pl.ds(k0, KEY_TILE)]
        v_tile = v_ref[pl.ds(k0, KEY_TILE), :]
        pvs = []
        for h in range(2):
            pv, carry = _sb_tile(qs[h][r0:], kt_tile, jnp.where(lanes_of[h], v_tile, 0), mm,
                                 carry_ref[h, r0:, :], causal)
            carry_ref[h, r0:, :] = carry
            pvs.append(pv)
        acc_ref[r0:, :] += pvs[0] + pvs[1]

    for d in reversed(range(tq // KEY_TILE)):
        r0 = d * KEY_TILE
        rows = lax.broadcasted_iota(jnp.int32, (tq - r0, KEY_TILE), 0)
        cols = lax.broadcasted_iota(jnp.int32, (tq - r0, KEY_TILE), 1)
        tile(pl.multiple_of(q0 + r0, KEY_TILE), r0, cols < rows)

    def off_diag(it, _):
        tile(pl.multiple_of((n_off - 1 - it) * KEY_TILE, KEY_TILE), 0, None)
        return 0

    lax.fori_loop(0, n_off, off_diag, 0)
    o_ref[...] = acc_ref[...].astype(o_ref.dtype)


def _sb_attention(qkv, kt, *, tq=512):
    b, s, n3 = qkv.shape
    mix = n3 // 3
    assert s % tq == 0 and tq % KEY_TILE == 0 and mix % LANES == 0
    v_col0 = 2 * mix // LANES
    mm = jnp.asarray(_suffix_matrix(), MXU_DTYPE)
    return pl.pallas_call(
        functools.partial(_sb_body, tq=tq),
        grid=(b, mix // LANES, s // tq),
        in_specs=[pl.BlockSpec((None, tq, LANES), lambda bi, hp, i: (bi, i, hp)),
                  pl.BlockSpec((None, LANES, s), lambda bi, hp, i: (bi, hp, 0)),
                  pl.BlockSpec((None, s, LANES), lambda bi, hp, i: (bi, 0, v_col0 + hp)),
                  _const_spec((2 * LANES, 2 * LANES))],
        out_specs=pl.BlockSpec((None, tq, LANES), lambda bi, hp, i: (bi, i, hp)),
        out_shape=jax.ShapeDtypeStruct((b, s, mix), MXU_DTYPE),
        scratch_shapes=[pltpu.VMEM((tq, LANES), F32), pltpu.VMEM((2, tq, LANES), F32)],
        compiler_params=_params(3),
        name="stick_breaking",
    )(qkv, kt, qkv, mm)


def _compress_body(c_ref, pe_ref, w1_ref, w2_ref, o_ref):
    c = c_ref[...]
    nch, half = c.shape
    top = _dot((c + pe_ref[0:1, :]).astype(MXU_DTYPE), w1_ref[:half, :])
    bot = _dot((c + pe_ref[1:2, :]).astype(MXU_DTYPE), w1_ref[half:, :])
    hid = top + pltpu.roll(bot, nch - 1, 0)
    o_ref[...] = _dot(jax.nn.gelu(hid).astype(MXU_DTYPE), w2_ref[...])


def _compress(chunks, pe, w1, w2):
    _, b, hk, nch, half = chunks.shape
    hid = w1.shape[-1]
    return pl.pallas_call(
        _compress_body,
        grid=(2, b, hk),
        in_specs=[pl.BlockSpec((None, None, None, nch, half), lambda c, bi, h: (c, bi, h, 0, 0)),
                  pl.BlockSpec((None, 2, half), lambda c, bi, h: (c, 0, 0)),
                  pl.BlockSpec((None, 2 * half, hid), lambda c, bi, h: (c, 0, 0)),
                  pl.BlockSpec((None, hid, HEAD_DIM), lambda c, bi, h: (c, 0, 0))],
        out_specs=pl.BlockSpec((None, None, None, nch, HEAD_DIM), lambda c, bi, h: (c, bi, h, 0, 0)),
        out_shape=jax.ShapeDtypeStruct((2, b, hk, nch, HEAD_DIM), F32),
        compiler_params=_params(3),
        name="nsa_compress",
    )(chunks, pe, w1, w2)


def _nsa_body(qs_ref, qt_ref, kc_ref, vct_ref, ovt_ref, bct_ref, kst_ref, vs_ref, bs_ref,
              kwt_ref, vw_ref, bw_ref, g_ref, o_ref, sc_ref, *, tq):
    i = pl.program_id(2)
    q0 = i * tq
    rows = GROUP * tq
    ncp = kc_ref.shape[0]
    nsel = ovt_ref.shape[0]
    band = bct_ref.shape[0]

    qt = jnp.concatenate([qt_ref[g] for g in range(GROUP)], axis=1)
    sc_ref[...] = _dot(kc_ref[...], qt)
    j0 = pl.multiple_of(i * (tq // CMP_STRIDE), CMP_STRIDE)
    sc_ref[pl.ds(j0, band), :] += bct_ref[...]
    jp = lax.broadcasted_iota(jnp.int32, (ncp, rows), 0)
    tcol = q0 + (lax.broadcasted_iota(jnp.int32, (ncp, rows), 1) & (tq - 1))
    valid = (jp >= CMP_PAD) & (CMP_STRIDE * (jp - CMP_PAD) + (CMP_LEN - 1) <= tcol)
    s = jnp.where(valid, sc_ref[...], -jnp.inf)
    m = jnp.max(s, axis=0, keepdims=True)
    e = jnp.exp(s - jnp.where(m > -jnp.inf, m, 0.0))
    den = jnp.sum(e, axis=0, keepdims=True)
    pc = (e / jnp.where(den > 0, den, 1.0)).astype(MXU_DTYPE)
    o_c = _dot(vct_ref[...], pc).T
    imp = _dot(ovt_ref[...], pc[:, :tq])
    for g in range(1, GROUP):
        imp = imp + _dot(ovt_ref[...], pc[:, g * tq:(g + 1) * tq])

    sid = lax.broadcasted_iota(jnp.int32, (nsel, tq), 0)
    cur = (q0 + lax.broadcasted_iota(jnp.int32, (nsel, tq), 1)) // SEL_LEN
    forced = (sid == 0) | (sid == cur) | (sid == cur - 1)
    v = jnp.where(forced, FORCED_SCORE, jnp.where(sid <= cur, imp, -1.0))
    mask_t = jnp.full((nsel, tq), NEG, F32)
    for _ in range(min(SEL_TOPK, nsel)):
        mx = jnp.max(v, axis=0, keepdims=True)
        idx = jnp.min(jnp.where(v == mx, sid, nsel), axis=0, keepdims=True)
        hit = sid == idx
        mask_t = jnp.where(hit, 0.0, mask_t)
        v = jnp.where(hit, -jnp.inf, v)
    mask = mask_t.T.astype(MXU_DTYPE)
    qs = qs_ref[...].reshape(rows, HEAD_DIM)
    qa = jnp.concatenate([jnp.concatenate([mask] * GROUP, axis=0), qs], axis=1)

    def far_tile(kt, carry):
        k0 = pl.multiple_of(WINDOW + kt * KEY_TILE, KEY_TILE)
        k_tile = kst_ref[:, pl.ds(k0, KEY_TILE)]
        v_tile = vs_ref[pl.ds(k0, KEY_TILE), :]
        out = []
        for g, (m_run, acc) in enumerate(carry):
            s_f = _dot(qa[g * tq:(g + 1) * tq], k_tile)
            m_new = jnp.maximum(m_run, jnp.max(s_f, axis=1, keepdims=True))
            p_f = jnp.exp(s_f - m_new).astype(MXU_DTYPE)
            out.append((m_new, jnp.exp(m_run - m_new) * acc + _dot(p_f, v_tile)))
        return tuple(out)

    near = bs_ref.shape[1]
    n_far = jnp.maximum(q0 - (near - tq), 0) // KEY_TILE
    init = (jnp.full((tq, 1), -jnp.inf, F32), jnp.zeros((tq, LANES), F32))
    parts = lax.fori_loop(0, n_far, far_tile, (init,) * GROUP)
    m_run = jnp.concatenate([p[0] for p in parts], axis=0)
    acc = jnp.concatenate([p[1] for p in parts], axis=0)
    k0 = pl.multiple_of(q0 + WINDOW - (near - tq), KEY_TILE)
    s_n = _dot(qa, kst_ref[:, pl.ds(k0, near)]) + bs_ref[...]
    col = lax.broadcasted_iota(jnp.int32, (rows, near), 1)
    s_n = jnp.where(q0 - (near - tq) + col >= 0, s_n, NEG)
    m_new = jnp.maximum(m_run, jnp.max(s_n, axis=1, keepdims=True))
    p_n = jnp.exp(s_n - m_new).astype(MXU_DTYPE)
    acc = jnp.exp(m_run - m_new) * acc + _dot(p_n, vs_ref[pl.ds(k0, near), :])
    o_s = acc[:, :HEAD_DIM] / acc[:, HEAD_DIM:HEAD_DIM + 1]

    slab = bw_ref.shape[1]
    kw0 = pl.multiple_of(q0, KEY_TILE)
    s_w = _dot(qs, kwt_ref[:, pl.ds(kw0, slab)]) + bw_ref[...]
    col = lax.broadcasted_iota(jnp.int32, (rows, slab), 1)
    s_w = jnp.where(q0 - WINDOW + col >= 0, s_w, NEG)
    p_w = jnp.exp(s_w - jnp.max(s_w, axis=1, keepdims=True)).astype(MXU_DTYPE)
    acc_w = _dot(p_w, vw_ref[pl.ds(kw0, slab), :])
    o_w = acc_w[:, :HEAD_DIM] / acc_w[:, HEAD_DIM:HEAD_DIM + 1]

    sg = jax.nn.sigmoid(g_ref[...])

    def gate(branch):
        return jnp.concatenate([sg[:, branch * GROUP + g:branch * GROUP + g + 1] for g in range(GROUP)], axis=0)

    out = gate(0) * o_c + gate(1) * o_s + gate(2) * o_w
    o_ref[...] = out.reshape(GROUP, tq, HEAD_DIM).astype(o_ref.dtype)


def _rel_bucket_table(n):
    d = np.arange(n)
    exact = REL_BUCKETS // 2
    nf = np.maximum(d, 1).astype(np.float32)
    large = exact + (np.log(nf / np.float32(exact)) / np.float32(math.log(REL_MAX_DIST / exact))
                     * np.float32(REL_BUCKETS - exact)).astype(np.int32)
    return np.where(d < exact, d, np.minimum(large, REL_BUCKETS - 1)).astype(np.int32)


def _bias_tile(rel_bias, dist, valid, shifted, invalid_value):
    bucket = _rel_bucket_table(int(dist.max()) + 1)[np.maximum(dist, 0)]
    tile = rel_bias[bucket]
    if shifted:
        tile = tile - rel_bias[REL_BUCKETS - 1]
    tile = jnp.where(valid[..., None], tile, invalid_value)
    tile = jnp.moveaxis(tile, -1, 0)
    return tile.reshape((N_KV_HEADS, GROUP) + dist.shape)


def _toeplitz_bias(rel_bias, n_rows, n_cols, off, valid, shifted, invalid_value):
    period = n_rows + n_cols
    x = np.arange(period)
    x = np.where(x < n_cols, x, x - period)
    dist = np.maximum(off - x, 0)
    u = rel_bias[_rel_bucket_table(int(dist.max()) + 1)[dist]]
    if shifted:
        u = u - rel_bias[REL_BUCKETS - 1]
    flat = jnp.tile(u.T, (1, n_rows))[:, :n_rows * (period - 1)]
    tile = flat.reshape(-1, n_rows, period - 1)[:, :, :n_cols]
    tile = jnp.where(valid[None], tile, invalid_value)
    return tile.reshape(N_KV_HEADS, GROUP * n_rows, n_cols)


def _nsa_attention(qs, qt, kc, vct, kst, vs, kwt, vw, gates, rel_bias, *, tq=256):
    b, nh, s, _ = qs.shape
    ncp = kc.shape[2]
    nsel = s // SEL_LEN
    sp = kst.shape[-1]
    rows = GROUP * tq
    near = 2 * tq
    slab = tq + WINDOW
    band = tq // CMP_STRIDE + 16
    assert s % tq == 0 and tq % KEY_TILE == 0 and tq & (tq - 1) == 0 and sp == s + WINDOW

    jc = np.arange(ncp)[None, :] - CMP_PAD
    cs = jc * CMP_STRIDE
    ss = np.arange(nsel)[:, None] * SEL_LEN
    ovt = ((jc >= 0) & (jc < (s - CMP_LEN) // CMP_STRIDE + 1) & (cs < ss + SEL_LEN) & (cs + CMP_LEN > ss))
    ovt = jnp.asarray(ovt.astype(np.float32), MXU_DTYPE)

    r = np.arange(tq)
    d_c = r[None, :] + (16 * CMP_STRIDE - (CMP_LEN - 1)) - CMP_STRIDE * np.arange(band)[:, None]
    bct = _bias_tile(rel_bias, d_c, d_c >= 0, True, 0.0)
    bct = bct.transpose(0, 2, 1, 3).reshape(N_KV_HEADS, band, rows)
    d_s = r[:, None] - np.arange(near)[None, :] + (near - tq)
    bs = _toeplitz_bias(rel_bias, tq, near, near - tq, d_s >= 0, True, NEG)
    d_w = r[:, None] - np.arange(slab)[None, :] + WINDOW
    bw = _toeplitz_bias(rel_bias, tq, slab, WINDOW, (d_w >= 0) & (d_w < WINDOW), False, NEG)

    per_head = lambda *shape: pl.BlockSpec((None, None) + shape, lambda bi, h, i: (bi, h, 0, 0))
    per_kvh = lambda *shape: pl.BlockSpec((None,) + shape, lambda bi, h, i: (h, 0, 0))
    return pl.pallas_call(
        functools.partial(_nsa_body, tq=tq),
        grid=(b, N_KV_HEADS, s // tq),
        in_specs=[pl.BlockSpec((None, GROUP, tq, HEAD_DIM), lambda bi, h, i: (bi, h, i, 0)),
                  pl.BlockSpec((None, GROUP, HEAD_DIM, tq), lambda bi, h, i: (bi, h, 0, i)),
                  per_head(ncp, HEAD_DIM),
                  per_head(HEAD_DIM, ncp),
                  _const_spec((nsel, ncp)),
                  per_kvh(band, rows),
                  per_head(nsel + HEAD_DIM, sp),
                  per_head(sp, LANES),
                  per_kvh(rows, near),
                  per_head(HEAD_DIM, sp),
                  per_head(sp, LANES),
                  per_kvh(rows, slab),
                  pl.BlockSpec((None, None, tq, 16), lambda bi, h, i: (bi, h, i, 0))],
        out_specs=pl.BlockSpec((None, GROUP, tq, HEAD_DIM), lambda bi, h, i: (bi, h, i, 0)),
        out_shape=jax.ShapeDtypeStruct((b, nh, s, HEAD_DIM), MXU_DTYPE),
        scratch_shapes=[pltpu.VMEM((ncp, rows), F32)],
        compiler_params=_params(3),
        name="nsa_attention",
    )(qs, qt, kc, vct, ovt, bct, kst, vs, bs, kwt, vw, bw, gates)


def _with_ones(v):
    ones = jnp.ones(v.shape[:-1] + (1,), v.dtype)
    zeros = jnp.zeros(v.shape[:-1] + (LANES - HEAD_DIM - 1,), v.dtype)
    return jnp.concatenate([v, ones, zeros], axis=-1)


def _pad_keys(x, axis):
    pad = [(0, 0)] * x.ndim
    pad[axis] = (WINDOW, 0)
    return jnp.pad(x, pad)


def _nsa_mixer(hn, w_in, rel_bias, pe_k, pe_v, ck_w1, ck_w2, cv_w1, cv_w2, norm_g, b, s):
    d = hn.shape[-1]
    mix = N_HEADS * HEAD_DIM
    kv_end = mix + 6 * N_KV_HEADS * HEAD_DIM
    n_in = w_in.shape[1]
    n_pad = -n_in % KEY_TILE
    w = jnp.pad(w_in, ((0, 0), (0, n_pad))).astype(MXU_DTYPE)
    proj = _norm_matmul(hn, norm_g, w, F32, col_chunk=KEY_TILE).reshape(b, s, n_in + n_pad)

    q = proj[..., :mix].reshape(b, s, N_HEADS, HEAD_DIM).transpose(0, 2, 1, 3)
    qs = (q * (HEAD_DIM ** -0.5)).astype(MXU_DTYPE)
    qt = qs.transpose(0, 1, 3, 2)
    kv = proj[..., mix:kv_end].reshape(b, s, 6, N_KV_HEADS, HEAD_DIM).transpose(2, 0, 3, 1, 4)
    gates = proj[..., kv_end:kv_end + 3 * N_HEADS].reshape(b, s, 3, N_KV_HEADS, GROUP)
    gates = gates.transpose(0, 3, 1, 2, 4).reshape(b, N_KV_HEADS, s, 3 * GROUP)
    gates = jnp.pad(gates, ((0, 0), (0, 0), (0, 0), (0, 16 - 3 * GROUP)))

    nch = s // CMP_STRIDE
    half = CMP_STRIDE * HEAD_DIM
    chunks = kv[0:2].reshape(2, b, N_KV_HEADS, nch, half)
    pe = jnp.stack([pe_k, pe_v]).reshape(2, 2, half)
    w1 = jnp.stack([ck_w1, cv_w1]).astype(MXU_DTYPE)
    w2 = jnp.stack([ck_w2, cv_w2]).astype(MXU_DTYPE)
    cmp = _compress(chunks, pe, w1, w2)
    ncp = -(-(nch + CMP_PAD) // LANES) * LANES
    cmp = jnp.pad(cmp, ((0, 0), (0, 0), (0, 0), (CMP_PAD, ncp - nch - CMP_PAD), (0, 0))).astype(MXU_DTYPE)
    kc = cmp[0]
    vct = cmp[1].transpose(0, 1, 3, 2)

    nsel = s // SEL_LEN
    onehot = (np.arange(s)[None, :] // SEL_LEN == np.arange(nsel)[:, None]).astype(np.float32)
    onehot = jnp.broadcast_to(jnp.asarray(onehot, MXU_DTYPE), (b, N_KV_HEADS, nsel, s))
    kst = jnp.concatenate([onehot, kv[2].astype(MXU_DTYPE).transpose(0, 1, 3, 2)], axis=2)
    kst = _pad_keys(kst, 3)
    vs = _pad_keys(_with_ones(kv[3].astype(MXU_DTYPE)), 2)
    kwt = _pad_keys(kv[4].astype(MXU_DTYPE).transpose(0, 1, 3, 2), 3)
    vw = _pad_keys(_with_ones(kv[5].astype(MXU_DTYPE)), 2)

    out = _nsa_attention(qs, qt, kc, vct, kst, vs, kwt, vw, gates, rel_bias)
    return out.transpose(0, 2, 1, 3).reshape(b * s, mix)


def _sb_mixer(hn, w_in, norm_g, b, s):
    mix = N_HEADS * HEAD_DIM
    qkv = _norm_matmul(hn, norm_g, w_in.astype(MXU_DTYPE), MXU_DTYPE).reshape(b, s, 3 * mix)
    kt = qkv[..., mix:2 * mix].transpose(0, 2, 1)
    return _sb_attention(qkv, kt).reshape(b * s, mix)


def kernel(x, p, rel_bias, norm_mix, norm_ffn, norm_ple, final_norm, sb_w_in, sb_w_out, nsa_w_in, nsa_w_out,
           nsa_pe_k, nsa_pe_v, nsa_ck_w1, nsa_ck_w2, nsa_cv_w1, nsa_cv_w2, ffn_w_in, ffn_w_out,
           ple_w_proj, ple_w_gate):
    b, s, d = x.shape
    depth = p.shape[0]
    h = x.reshape(b * s, d)
    for i in range(depth):
        j = i // 2
        if i % 2 == 0:
            attn = _sb_mixer(h, sb_w_in[j], norm_mix[i], b, s)
            wo = sb_w_out[j]
        else:
            attn = _nsa_mixer(h, nsa_w_in[j], rel_bias, nsa_pe_k[j], nsa_pe_v[j], nsa_ck_w1[j], nsa_ck_w2[j],
                              nsa_cv_w1[j], nsa_cv_w2[j], norm_mix[i], b, s)
            wo = nsa_w_out[j]
        h = _post_mixer(h, attn, wo.astype(MXU_DTYPE), norm_ffn[i], ffn_w_in[i].astype(MXU_DTYPE),
                        ffn_w_out[i].astype(MXU_DTYPE), norm_ple[i], ple_w_gate[i].astype(MXU_DTYPE),
                        p[i].reshape(b * s, -1), ple_w_proj[i].astype(MXU_DTYPE), final_norm,
                        final=(i == depth - 1))
    return h.reshape(b, s, d)
```

```python
import functools
import math

import numpy as np
import jax
import jax.numpy as jnp
from jax import lax
from jax.experimental import pallas as pl
from jax.experimental.pallas import tpu as pltpu

F32 = jnp.float32
MXU_DTYPE = jnp.bfloat16

N_HEADS = 16
HEAD_DIM = 64
N_KV_HEADS = 4
GROUP = N_HEADS // N_KV_HEADS
REL_BUCKETS = 32
REL_MAX_DIST = 128
CMP_LEN = 32
CMP_STRIDE = 16
SEL_LEN = 64
SEL_TOPK = 16
WINDOW = 512
FORCED_SCORE = 100.0
EPS = 1e-6
NEG = -1e9
LOG2E = 1.4426950408889634
MAX_EXP2 = 126.0

LANES = 128
KEY_TILE = 256
CMP_PAD = 16
PAD_ROWS = 16
VMEM_LIMIT = 56 * 1024 * 1024


def _dot(a, b):
    return jnp.dot(a, b, preferred_element_type=F32)


def _rms(x, g):
    ms = jnp.mean(x * x, axis=-1, keepdims=True)
    return x * lax.rsqrt(ms + EPS) * g


def _const_spec(shape):
    nd = len(shape)
    return pl.BlockSpec(shape, lambda *_: (0,) * nd, pipeline_mode=pl.Buffered(1))


def _params(n_axes):
    return pltpu.CompilerParams(dimension_semantics=("arbitrary",) * n_axes,
                                vmem_limit_bytes=VMEM_LIMIT)


def _norm_matmul_body(x_ref, g_ref, w_ref, o_ref, *, col_chunk):
    xn = _rms(x_ref[...], g_ref[...]).astype(MXU_DTYPE)
    for c in range(0, o_ref.shape[-1], col_chunk):
        o_ref[:, c:c + col_chunk] = _dot(xn, w_ref[:, c:c + col_chunk]).astype(o_ref.dtype)


def _norm_matmul(x, g, w, out_dtype, *, tm=512, col_chunk=512):
    t, d = x.shape
    n = w.shape[1]
    assert t % tm == 0 and n % col_chunk == 0
    return pl.pallas_call(
        functools.partial(_norm_matmul_body, col_chunk=col_chunk),
        grid=(t // tm,),
        in_specs=[pl.BlockSpec((tm, d), lambda i: (i, 0)),
                  _const_spec((1, d)),
                  _const_spec((d, n))],
        out_specs=pl.BlockSpec((tm, n), lambda i: (i, 0)),
        out_shape=jax.ShapeDtypeStruct((t, n), out_dtype),
        compiler_params=_params(1),
        name="norm_matmul",
    )(x, g.reshape(1, d), w)


def _post_mixer_body(h_ref, a_ref, wo_ref, gf_ref, win_ref, wout_ref, gp_ref, wg_ref, p_ref, wp_ref,
                     gfin_ref, o_ref, *, d_ff, chunk, final):
    h1 = h_ref[...] + _dot(a_ref[...], wo_ref[...])
    xn = _rms(h1, gf_ref[...]).astype(MXU_DTYPE)
    acc = jnp.zeros_like(h1)
    for c in range(0, d_ff, chunk):
        a = _dot(xn, win_ref[:, c:c + chunk])
        b = _dot(xn, win_ref[:, d_ff + c:d_ff + c + chunk])
        act = a * jax.nn.sigmoid(a) * b
        acc = acc + _dot(act.astype(MXU_DTYPE), wout_ref[c:c + chunk, :])
    h2 = h1 + acc
    xg = _rms(h2, gp_ref[...]).astype(MXU_DTYPE)
    gate = jax.nn.sigmoid(_dot(xg, wg_ref[...]))
    h3 = h2 + gate * _dot(p_ref[...].astype(MXU_DTYPE), wp_ref[...])
    if final:
        h3 = _rms(h3, gfin_ref[...])
    o_ref[...] = h3


def _post_mixer(h, attn, wo, g_ffn, w_in, w_out, g_ple, wg, p, wp, g_final, *, final, tm=512, chunk=256):
    t, d = h.shape
    d_ff = w_out.shape[0]
    ple = p.shape[1]
    assert t % tm == 0 and d_ff % chunk == 0
    row = lambda i: (i, 0)
    return pl.pallas_call(
        functools.partial(_post_mixer_body, d_ff=d_ff, chunk=chunk, final=final),
        grid=(t // tm,),
        in_specs=[pl.BlockSpec((tm, d), row),
                  pl.BlockSpec((tm, d), row),
                  _const_spec((d, d)),
                  _const_spec((1, d)),
                  _const_spec((d, 2 * d_ff)),
                  _const_spec((d_ff, d)),
                  _const_spec((1, d)),
                  _const_spec((d, d)),
                  pl.BlockSpec((tm, ple), row),
                  _const_spec((ple, d)),
                  _const_spec((1, d))],
        out_specs=pl.BlockSpec((tm, d), row),
        out_shape=jax.ShapeDtypeStruct((t, d), F32),
        compiler_params=_params(1),
        name="post_mixer",
    )(h, attn, wo, g_ffn.reshape(1, d), w_in, w_out, g_ple.reshape(1, d), wg, p, wp,
      g_final.reshape(1, d))


def _suffix_matrix():
    j = np.arange(KEY_TILE)[:, None]
    c = np.arange(KEY_TILE)[None, :]
    return (j > c).astype(np.float32)


def _sb_tile(q, kt_tile, v_tile, mm, carry, causal):
    z = _dot(q, kt_tile) * LOG2E
    sp = jnp.maximum(z, jnp.log2(1.0 + jnp.exp2(jnp.minimum(z, MAX_EXP2))))
    if causal is not None:
        sp = jnp.where(causal, sp, 0.0)
    spb = sp.astype(MXU_DTYPE)
    r = _dot(spb, mm)
    cum = r + jnp.concatenate([carry] * (KEY_TILE // LANES), axis=1)
    carry = carry + (r[:, 0:1] + spb[:, 0:1].astype(F32))
    a = jnp.exp2(z - sp - cum)
    if causal is not None:
        a = jnp.where(causal, a, 0.0)
    return _dot(a.astype(MXU_DTYPE), v_tile), carry


def _sb_body(q_ref, kt_ref, v_ref, mm_ref, o_ref, acc_ref, carry_ref, *, tq):
    q0 = pl.program_id(2) * tq
    n_off = q0 // KEY_TILE
    mm = mm_ref[...]
    first = lax.broadcasted_iota(jnp.int32, (1, LANES), 1) < HEAD_DIM
    lanes_of = (first, jnp.logical_not(first))
    q2 = q_ref[...] * (HEAD_DIM ** -0.5)
    qs = [jnp.where(m, q2, 0).astype(MXU_DTYPE) for m in lanes_of]
    acc_ref[...] = jnp.zeros_like(acc_ref)
    carry_ref[...] = jnp.zeros_like(carry_ref)

    def tile(k0, r0, causal):
        kt_tile = kt_ref[:, pl.ds(k0, KEY_TILE)]
        v_tile = v_ref[pl.ds(k0, KEY_TILE), :]
        pvs = []
        for h in range(2):
            pv, carry = _sb_tile(qs[h][r0:], kt_tile, jnp.where(lanes_of[h], v_tile, 0), mm,
                                 carry_ref[h, r0:, :], causal)
            carry_ref[h, r0:, :] = carry
            pvs.append(pv)
        acc_ref[r0:, :] += pvs[0] + pvs[1]

    for d in reversed(range(tq // KEY_TILE)):
        r0 = d * KEY_TILE
        rows = lax.broadcasted_iota(jnp.int32, (tq - r0, KEY_TILE), 0)
        cols = lax.broadcasted_iota(jnp.int32, (tq - r0, KEY_TILE), 1)
        tile(pl.multiple_of(q0 + r0, KEY_TILE), r0, cols < rows)

    def off_diag(it, _):
        tile(pl.multiple_of((n_off - 1 - it) * KEY_TILE, KEY_TILE), 0, None)
        return 0

    lax.fori_loop(0, n_off, off_diag, 0)
    o_ref[...] = acc_ref[...].astype(o_ref.dtype)


def _sb_attention(qkv, kt, *, tq=512):
    b, s, n3 = qkv.shape
    mix = n3 // 3
    assert s % tq == 0 and tq % KEY_TILE == 0 and mix % LANES == 0
    v_col0 = 2 * mix // LANES
    mm = jnp.asarray(_suffix_matrix(), MXU_DTYPE)
    return pl.pallas_call(
        functools.partial(_sb_body, tq=tq),
        grid=(b, mix // LANES, s // tq),
        in_specs=[pl.BlockSpec((None, tq, LANES), lambda bi, hp, i: (bi, i, hp)),
                  pl.BlockSpec((None, LANES, s), lambda bi, hp, i: (bi, hp, 0)),
                  pl.BlockSpec((None, s, LANES), lambda bi, hp, i: (bi, 0, v_col0 + hp)),
                  _const_spec((2 * LANES, 2 * LANES))],
        out_specs=pl.BlockSpec((None, tq, LANES), lambda bi, hp, i: (bi, i, hp)),
        out_shape=jax.ShapeDtypeStruct((b, s, mix), MXU_DTYPE),
        scratch_shapes=[pltpu.VMEM((tq, LANES), F32), pltpu.VMEM((2, tq, LANES), F32)],
        compiler_params=_params(3),
        name="stick_breaking",
    )(qkv, kt, qkv, mm)


def _compress_body(c_ref, pe_ref, w1_ref, w2_ref, o_ref):
    c = c_ref[...]
    nch, half = c.shape
    top = _dot((c + pe_ref[0:1, :]).astype(MXU_DTYPE), w1_ref[:half, :])
    bot = _dot((c + pe_ref[1:2, :]).astype(MXU_DTYPE), w1_ref[half:, :])
    hid = top + pltpu.roll(bot, nch - 1, 0)
    o_ref[...] = _dot(jax.nn.gelu(hid).astype(MXU_DTYPE), w2_ref[...])


def _compress(chunks, pe, w1, w2):
    _, b, hk, nch, half = chunks.shape
    hid = w1.shape[-1]
    return pl.pallas_call(
        _compress_body,
        grid=(2, b, hk),
        in_specs=[pl.BlockSpec((None, None, None, nch, half), lambda c, bi, h: (c, bi, h, 0, 0)),
                  pl.BlockSpec((None, 2, half), lambda c, bi, h: (c, 0, 0)),
                  pl.BlockSpec((None, 2 * half, hid), lambda c, bi, h: (c, 0, 0)),
                  pl.BlockSpec((None, hid, HEAD_DIM), lambda c, bi, h: (c, 0, 0))],
        out_specs=pl.BlockSpec((None, None, None, nch, HEAD_DIM), lambda c, bi, h: (c, bi, h, 0, 0)),
        out_shape=jax.ShapeDtypeStruct((2, b, hk, nch, HEAD_DIM), F32),
        compiler_params=_params(3),
        name="nsa_compress",
    )(chunks, pe, w1, w2)


def _nsa_body(qs_ref, qt_ref, kc_ref, vct_ref, ovt_ref, bct_ref, kst_ref, vs_ref, bs_ref,
              kwt_ref, vw_ref, bw_ref, g_ref, o_ref, sc_ref, *, tq):
    i = pl.program_id(2)
    q0 = i * tq
    rows = GROUP * tq
    ncp = kc_ref.shape[0]
    nsel = ovt_ref.shape[0]
    band = bct_ref.shape[0]

    qt = jnp.concatenate([qt_ref[g] for g in range(GROUP)], axis=1)
    sc_ref[...] = _dot(kc_ref[...], qt)
    j0 = pl.multiple_of(i * (tq // CMP_STRIDE), CMP_STRIDE)
    sc_ref[pl.ds(j0, band), :] += bct_ref[...]
    jp = lax.broadcasted_iota(jnp.int32, (ncp, rows), 0)
    tcol = q0 + (lax.broadcasted_iota(jnp.int32, (ncp, rows), 1) & (tq - 1))
    valid = (jp >= CMP_PAD) & (CMP_STRIDE * (jp - CMP_PAD) + (CMP_LEN - 1) <= tcol)
    s = jnp.where(valid, sc_ref[...], -jnp.inf)
    m = jnp.max(s, axis=0, keepdims=True)
    e = jnp.exp(s - jnp.where(m > -jnp.inf, m, 0.0))
    den = jnp.sum(e, axis=0, keepdims=True)
    pc = (e / jnp.where(den > 0, den, 1.0)).astype(MXU_DTYPE)
    o_c = _dot(vct_ref[...], pc).T
    imp = _dot(ovt_ref[...], pc[:, :tq])
    for g in range(1, GROUP):
        imp = imp + _dot(ovt_ref[...], pc[:, g * tq:(g + 1) * tq])

    sid = lax.broadcasted_iota(jnp.int32, (nsel, tq), 0)
    cur = (q0 + lax.broadcasted_iota(jnp.int32, (nsel, tq), 1)) // SEL_LEN
    forced = (sid == 0) | (sid == cur) | (sid == cur - 1)
    v = jnp.where(forced, FORCED_SCORE, jnp.where(sid <= cur, imp, -1.0))
    mask_t = jnp.full((nsel, tq), NEG, F32)
    for _ in range(min(SEL_TOPK, nsel)):
        mx = jnp.max(v, axis=0, keepdims=True)
        idx = jnp.min(jnp.where(v == mx, sid, nsel), axis=0, keepdims=True)
        hit = sid == idx
        mask_t = jnp.where(hit, 0.0, mask_t)
        v = jnp.where(hit, -jnp.inf, v)
    mask = mask_t.T.astype(MXU_DTYPE)
    qs = qs_ref[...].reshape(rows, HEAD_DIM)
    pad_col = jnp.where(lax.broadcasted_iota(jnp.int32, (rows, PAD_ROWS), 1) == 0, NEG, 0.0).astype(MXU_DTYPE)
    qa = jnp.concatenate([jnp.concatenate([mask] * GROUP, axis=0), qs, pad_col], axis=1)

    near = bs_ref.shape[1]

    def scores(k0):
        return _dot(qa, kst_ref[:, pl.ds(pl.multiple_of(k0, KEY_TILE), near)])

    def far_tile(it, carry):
        m_run, acc, s_f = carry
        k0 = far0 + it * near
        s_next = scores(k0 + near)
        m_new = jnp.maximum(m_run, jnp.max(s_f, axis=1, keepdims=True))
        p_f = jnp.exp(s_f - m_new).astype(MXU_DTYPE)
        acc = jnp.exp(m_run - m_new) * acc + _dot(p_f, vs_ref[pl.ds(pl.multiple_of(k0, KEY_TILE), near), :])
        return m_new, acc, s_next

    near0 = q0 + WINDOW - (near - tq)
    n_far = (jnp.maximum(q0 - (near - tq), 0) + near - 1) // near
    far0 = near0 - n_far * near
    m_run, acc, s_n = lax.fori_loop(
        0, n_far, far_tile,
        (jnp.full((rows, 1), -jnp.inf, F32), jnp.zeros((rows, LANES), F32), scores(far0)))
    k0 = pl.multiple_of(near0, KEY_TILE)
    s_n = s_n + bs_ref[...]
    m_new = jnp.maximum(m_run, jnp.max(s_n, axis=1, keepdims=True))
    p_n = jnp.exp(s_n - m_new).astype(MXU_DTYPE)
    acc = jnp.exp(m_run - m_new) * acc + _dot(p_n, vs_ref[pl.ds(k0, near), :])
    o_s = acc[:, :HEAD_DIM] / acc[:, HEAD_DIM:HEAD_DIM + 1]

    slab = bw_ref.shape[1]
    kw0 = pl.multiple_of(q0, KEY_TILE)
    s_w = _dot(qs, kwt_ref[:, pl.ds(kw0, slab)]) + bw_ref[...]
    col = lax.broadcasted_iota(jnp.int32, (rows, slab), 1)
    s_w = jnp.where(q0 - WINDOW + col >= 0, s_w, NEG)
    p_w = jnp.exp(s_w - jnp.max(s_w, axis=1, keepdims=True)).astype(MXU_DTYPE)
    acc_w = _dot(p_w, vw_ref[pl.ds(kw0, slab), :])
    o_w = acc_w[:, :HEAD_DIM] / acc_w[:, HEAD_DIM:HEAD_DIM + 1]

    sg = jax.nn.sigmoid(g_ref[...])

    def gate(branch):
        return jnp.concatenate([sg[:, branch * GROUP + g:branch * GROUP + g + 1] for g in range(GROUP)], axis=0)

    out = gate(0) * o_c + gate(1) * o_s + gate(2) * o_w
    o_ref[...] = out.reshape(GROUP, tq, HEAD_DIM).astype(o_ref.dtype)


def _rel_bucket_table(n):
    d = np.arange(n)
    exact = REL_BUCKETS // 2
    nf = np.maximum(d, 1).astype(np.float32)
    large = exact + (np.log(nf / np.float32(exact)) / np.float32(math.log(REL_MAX_DIST / exact))
                     * np.float32(REL_BUCKETS - exact)).astype(np.int32)
    return np.where(d < exact, d, np.minimum(large, REL_BUCKETS - 1)).astype(np.int32)


def _bias_tile(rel_bias, dist, valid, shifted, invalid_value):
    bucket = _rel_bucket_table(int(dist.max()) + 1)[np.maximum(dist, 0)]
    tile = rel_bias[bucket]
    if shifted:
        tile = tile - rel_bias[REL_BUCKETS - 1]
    tile = jnp.where(valid[..., None], tile, invalid_value)
    tile = jnp.moveaxis(tile, -1, 0)
    return tile.reshape((N_KV_HEADS, GROUP) + dist.shape)


def _toeplitz_bias(rel_bias, n_rows, n_cols, off, valid, shifted, invalid_value):
    period = n_rows + n_cols
    x = np.arange(period)
    x = np.where(x < n_cols, x, x - period)
    dist = np.maximum(off - x, 0)
    u = rel_bias[_rel_bucket_table(int(dist.max()) + 1)[dist]]
    if shifted:
        u = u - rel_bias[REL_BUCKETS - 1]
    flat = jnp.tile(u.T, (1, n_rows))[:, :n_rows * (period - 1)]
    tile = flat.reshape(-1, n_rows, period - 1)[:, :, :n_cols]
    tile = jnp.where(valid[None], tile, invalid_value)
    return tile.reshape(N_KV_HEADS, GROUP * n_rows, n_cols)


def _nsa_attention(qs, qt, kc, vct, kst, vs, kwt, vw, gates, rel_bias, *, tq=256):
    b, nh, s, _ = qs.shape
    ncp = kc.shape[2]
    nsel = s // SEL_LEN
    sp = kst.shape[-1]
    rows = GROUP * tq
    near = 2 * tq
    slab = tq + WINDOW
    band = tq // CMP_STRIDE + 16
    assert s % tq == 0 and tq % KEY_TILE == 0 and tq & (tq - 1) == 0 and sp == s + WINDOW

    jc = np.arange(ncp)[None, :] - CMP_PAD
    cs = jc * CMP_STRIDE
    ss = np.arange(nsel)[:, None] * SEL_LEN
    ovt = ((jc >= 0) & (jc < (s - CMP_LEN) // CMP_STRIDE + 1) & (cs < ss + SEL_LEN) & (cs + CMP_LEN > ss))
    ovt = jnp.asarray(ovt.astype(np.float32), MXU_DTYPE)

    r = np.arange(tq)
    d_c = r[None, :] + (16 * CMP_STRIDE - (CMP_LEN - 1)) - CMP_STRIDE * np.arange(band)[:, None]
    bct = _bias_tile(rel_bias, d_c, d_c >= 0, True, 0.0)
    bct = bct.transpose(0, 2, 1, 3).reshape(N_KV_HEADS, band, rows)
    d_s = r[:, None] - np.arange(near)[None, :] + (near - tq)
    bs = _toeplitz_bias(rel_bias, tq, near, near - tq, d_s >= 0, True, NEG)
    d_w = r[:, None] - np.arange(slab)[None, :] + WINDOW
    bw = _toeplitz_bias(rel_bias, tq, slab, WINDOW, (d_w >= 0) & (d_w < WINDOW), False, NEG)

    per_head = lambda *shape: pl.BlockSpec((None, None) + shape, lambda bi, h, i: (bi, h, 0, 0))
    per_kvh = lambda *shape: pl.BlockSpec((None,) + shape, lambda bi, h, i: (h, 0, 0))
    return pl.pallas_call(
        functools.partial(_nsa_body, tq=tq),
        grid=(b, N_KV_HEADS, s // tq),
        in_specs=[pl.BlockSpec((None, GROUP, tq, HEAD_DIM), lambda bi, h, i: (bi, h, i, 0)),
                  pl.BlockSpec((None, GROUP, HEAD_DIM, tq), lambda bi, h, i: (bi, h, 0, i)),
                  per_head(ncp, HEAD_DIM),
                  per_head(HEAD_DIM, ncp),
                  _const_spec((nsel, ncp)),
                  per_kvh(band, rows),
                  per_head(nsel + HEAD_DIM + PAD_ROWS, sp),
                  per_head(sp, LANES),
                  per_kvh(rows, near),
                  per_head(HEAD_DIM, sp),
                  per_head(sp, LANES),
                  per_kvh(rows, slab),
                  pl.BlockSpec((None, None, tq, 16), lambda bi, h, i: (bi, h, i, 0))],
        out_specs=pl.BlockSpec((None, GROUP, tq, HEAD_DIM), lambda bi, h, i: (bi, h, i, 0)),
        out_shape=jax.ShapeDtypeStruct((b, nh, s, HEAD_DIM), MXU_DTYPE),
        scratch_shapes=[pltpu.VMEM((ncp, rows), F32)],
        compiler_params=_params(3),
        name="nsa_attention",
    )(qs, qt, kc, vct, ovt, bct, kst, vs, bs, kwt, vw, bw, gates)


def _with_ones(v):
    ones = jnp.ones(v.shape[:-1] + (1,), v.dtype)
    zeros = jnp.zeros(v.shape[:-1] + (LANES - HEAD_DIM - 1,), v.dtype)
    return jnp.concatenate([v, ones, zeros], axis=-1)


def _pad_keys(x, axis):
    pad = [(0, 0)] * x.ndim
    pad[axis] = (WINDOW, 0)
    return jnp.pad(x, pad)


def _nsa_mixer(hn, w_in, rel_bias, pe_k, pe_v, ck_w1, ck_w2, cv_w1, cv_w2, norm_g, b, s):
    d = hn.shape[-1]
    mix = N_HEADS * HEAD_DIM
    kv_end = mix + 6 * N_KV_HEADS * HEAD_DIM
    n_in = w_in.shape[1]
    n_pad = -n_in % KEY_TILE
    w = jnp.pad(w_in, ((0, 0), (0, n_pad))).astype(MXU_DTYPE)
    proj = _norm_matmul(hn, norm_g, w, F32, col_chunk=KEY_TILE).reshape(b, s, n_in + n_pad)

    q = proj[..., :mix].reshape(b, s, N_HEADS, HEAD_DIM).transpose(0, 2, 1, 3)
    qs = (q * (HEAD_DIM ** -0.5)).astype(MXU_DTYPE)
    qt = qs.transpose(0, 1, 3, 2)
    kv = proj[..., mix:kv_end].reshape(b, s, 6, N_KV_HEADS, HEAD_DIM).transpose(2, 0, 3, 1, 4)
    gates = proj[..., kv_end:kv_end + 3 * N_HEADS].reshape(b, s, 3, N_KV_HEADS, GROUP)
    gates = gates.transpose(0, 3, 1, 2, 4).reshape(b, N_KV_HEADS, s, 3 * GROUP)
    gates = jnp.pad(gates, ((0, 0), (0, 0), (0, 0), (0, 16 - 3 * GROUP)))

    nch = s // CMP_STRIDE
    half = CMP_STRIDE * HEAD_DIM
    chunks = kv[0:2].reshape(2, b, N_KV_HEADS, nch, half)
    pe = jnp.stack([pe_k, pe_v]).reshape(2, 2, half)
    w1 = jnp.stack([ck_w1, cv_w1]).astype(MXU_DTYPE)
    w2 = jnp.stack([ck_w2, cv_w2]).astype(MXU_DTYPE)
    cmp = _compress(chunks, pe, w1, w2)
    ncp = -(-(nch + CMP_PAD) // LANES) * LANES
    cmp = jnp.pad(cmp, ((0, 0), (0, 0), (0, 0), (CMP_PAD, ncp - nch - CMP_PAD), (0, 0))).astype(MXU_DTYPE)
    kc = cmp[0]
    vct = cmp[1].transpose(0, 1, 3, 2)

    nsel = s // SEL_LEN
    onehot = (np.arange(s)[None, :] // SEL_LEN == np.arange(nsel)[:, None]).astype(np.float32)
    onehot = jnp.broadcast_to(jnp.asarray(onehot, MXU_DTYPE), (b, N_KV_HEADS, nsel, s))
    kst = _pad_keys(jnp.concatenate([onehot, kv[2].astype(MXU_DTYPE).transpose(0, 1, 3, 2)], axis=2), 3)
    is_pad = (np.arange(s + WINDOW)[None, :] < WINDOW) & (np.arange(PAD_ROWS)[:, None] == 0)
    is_pad = jnp.broadcast_to(jnp.asarray(is_pad.astype(np.float32), MXU_DTYPE), kst.shape[:2] + is_pad.shape)
    kst = jnp.concatenate([kst, is_pad], axis=2)
    vs = _pad_keys(_with_ones(kv[3].astype(MXU_DTYPE)), 2)
    kwt = _pad_keys(kv[4].astype(MXU_DTYPE).transpose(0, 1, 3, 2), 3)
    vw = _pad_keys(_with_ones(kv[5].astype(MXU_DTYPE)), 2)

    out = _nsa_attention(qs, qt, kc, vct, kst, vs, kwt, vw, gates, rel_bias)
    return out.transpose(0, 2, 1, 3).reshape(b * s, mix)


def _sb_mixer(hn, w_in, norm_g, b, s):
    mix = N_HEADS * HEAD_DIM
    qkv = _norm_matmul(hn, norm_g, w_in.astype(MXU_DTYPE), MXU_DTYPE).reshape(b, s, 3 * mix)
    kt = qkv[..., mix:2 * mix].transpose(0, 2, 1)
    return _sb_attention(qkv, kt).reshape(b * s, mix)


def kernel(x, p, rel_bias, norm_mix, norm_ffn, norm_ple, final_norm, sb_w_in, sb_w_out, nsa_w_in, nsa_w_out,
           nsa_pe_k, nsa_pe_v, nsa_ck_w1, nsa_ck_w2, nsa_cv_w1, nsa_cv_w2, ffn_w_in, ffn_w_out,
           ple_w_proj, ple_w_gate):
    b, s, d = x.shape
    depth = p.shape[0]
    h = x.reshape(b * s, d)
    for i in range(depth):
        j = i // 2
        if i % 2 == 0:
            attn = _sb_mixer(h, sb_w_in[j], norm_mix[i], b, s)
            wo = sb_w_out[j]
        else:
            attn = _nsa_mixer(h, nsa_w_in[j], rel_bias, nsa_pe_k[j], nsa_pe_v[j], nsa_ck_w1[j], nsa_ck_w2[j],
                              nsa_cv_w1[j], nsa_cv_w2[j], norm_mix[i], b, s)
            wo = nsa_w_out[j]
        h = _post_mixer(h, attn, wo.astype(MXU_DTYPE), norm_ffn[i], ffn_w_in[i].astype(MXU_DTYPE),
                        ffn_w_out[i].astype(MXU_DTYPE), norm_ple[i], ple_w_gate[i].astype(MXU_DTYPE),
                        p[i].reshape(b * s, -1), ple_w_proj[i].astype(MXU_DTYPE), final_norm,
                        final=(i == depth - 1))
    return h.reshape(b, s, d)
```

```python
import functools
import math

import numpy as np
import jax
import jax.numpy as jnp
from jax import lax
from jax.experimental import pallas as pl
from jax.experimental.pallas import tpu as pltpu

F32 = jnp.float32
MXU_DTYPE = jnp.bfloat16

N_HEADS = 16
HEAD_DIM = 64
N_KV_HEADS = 4
GROUP = N_HEADS // N_KV_HEADS
REL_BUCKETS = 32
REL_MAX_DIST = 128
CMP_LEN = 32
CMP_STRIDE = 16
SEL_LEN = 64
SEL_TOPK = 16
WINDOW = 512
FORCED_SCORE = 100.0
EPS = 1e-6
NEG = -1e9
LOG2E = 1.4426950408889634
Q_SCALE = HEAD_DIM ** -0.5 * LOG2E
MAX_EXP2 = 126.0

LANES = 128
KEY_TILE = 256
CMP_BAND_BACK = 16
SB_UNROLL = 2
PAD_ROWS = 16
VMEM_LIMIT = 56 * 1024 * 1024


def _dot(a, b):
    return jnp.dot(a, b, preferred_element_type=F32)


def _rms(x, g):
    ms = jnp.mean(x * x, axis=-1, keepdims=True)
    return x * lax.rsqrt(ms + EPS) * g


def _const_spec(shape):
    nd = len(shape)
    return pl.BlockSpec(shape, lambda *_: (0,) * nd, pipeline_mode=pl.Buffered(1))


def _params(n_axes):
    return pltpu.CompilerParams(dimension_semantics=("arbitrary",) * n_axes,
                                vmem_limit_bytes=VMEM_LIMIT)


def _norm_matmul_body(x_ref, g_ref, w_ref, cs_ref, *o_refs, col_chunk):
    xn = _rms(x_ref[...], g_ref[...]).astype(MXU_DTYPE)
    c0 = 0
    for o_ref in o_refs:
        for c in range(0, o_ref.shape[-1], col_chunk):
            cols = slice(c0 + c, c0 + c + col_chunk)
            o_ref[:, c:c + col_chunk] = (_dot(xn, w_ref[:, cols]) * cs_ref[:, cols]).astype(o_ref.dtype)
        c0 += o_ref.shape[-1]


def _norm_matmul(x, g, w, col_scale, outs, *, tm=512, col_chunk=512):
    t, d = x.shape
    n = w.shape[1]
    assert t % tm == 0 and sum(width for width, _ in outs) == n
    assert all(width % col_chunk == 0 for width, _ in outs)
    return pl.pallas_call(
        functools.partial(_norm_matmul_body, col_chunk=col_chunk),
        grid=(t // tm,),
        in_specs=[pl.BlockSpec((tm, d), lambda i: (i, 0)),
                  _const_spec((1, d)),
                  _const_spec((d, n)),
                  _const_spec((1, n))],
        out_specs=[pl.BlockSpec((tm, width), lambda i: (i, 0)) for width, _ in outs],
        out_shape=[jax.ShapeDtypeStruct((t, width), dtype) for width, dtype in outs],
        compiler_params=_params(1),
        name="norm_matmul",
    )(x, g.reshape(1, d), w, col_scale.reshape(1, n))


def _post_mixer_body(h_ref, a_ref, wo_ref, gf_ref, win_ref, wout_ref, gp_ref, wg_ref, p_ref, wp_ref,
                     gfin_ref, o_ref, *, d_ff, chunk, final):
    h1 = h_ref[...] + _dot(a_ref[...], wo_ref[...])
    xn = _rms(h1, gf_ref[...]).astype(MXU_DTYPE)
    acc = jnp.zeros_like(h1)
    for c in range(0, d_ff, chunk):
        a = _dot(xn, win_ref[:, c:c + chunk])
        b = _dot(xn, win_ref[:, d_ff + c:d_ff + c + chunk])
        act = a * jax.nn.sigmoid(a) * b
        acc = acc + _dot(act.astype(MXU_DTYPE), wout_ref[c:c + chunk, :])
    h2 = h1 + acc
    xg = _rms(h2, gp_ref[...]).astype(MXU_DTYPE)
    gate = jax.nn.sigmoid(_dot(xg, wg_ref[...]))
    h3 = h2 + gate * _dot(p_ref[...].astype(MXU_DTYPE), wp_ref[...])
    if final:
        h3 = _rms(h3, gfin_ref[...])
    o_ref[...] = h3


def _post_mixer(h, attn, wo, g_ffn, w_in, w_out, g_ple, wg, p, wp, g_final, *, final, tm=512, chunk=256):
    t, d = h.shape
    d_ff = w_out.shape[0]
    ple = p.shape[1]
    assert t % tm == 0 and d_ff % chunk == 0
    row = lambda i: (i, 0)
    return pl.pallas_call(
        functools.partial(_post_mixer_body, d_ff=d_ff, chunk=chunk, final=final),
        grid=(t // tm,),
        in_specs=[pl.BlockSpec((tm, d), row),
                  pl.BlockSpec((tm, d), row),
                  _const_spec((d, d)),
                  _const_spec((1, d)),
                  _const_spec((d, 2 * d_ff)),
                  _const_spec((d_ff, d)),
                  _const_spec((1, d)),
                  _const_spec((d, d)),
                  pl.BlockSpec((tm, ple), row),
                  _const_spec((ple, d)),
                  _const_spec((1, d))],
        out_specs=pl.BlockSpec((tm, d), row),
        out_shape=jax.ShapeDtypeStruct((t, d), F32),
        compiler_params=_params(1),
        name="post_mixer",
    )(h, attn, wo, g_ffn.reshape(1, d), w_in, w_out, g_ple.reshape(1, d), wg, p, wp,
      g_final.reshape(1, d))


def _suffix_matrix():
    j = np.arange(KEY_TILE)[:, None]
    c = np.arange(KEY_TILE)[None, :]
    return (j > c).astype(np.float32)


def _sb_tile(q, kt_tile, v_tile, mm, carry, causal):
    z = _dot(q, kt_tile)
    sp = jnp.maximum(z, jnp.log2(1.0 + jnp.exp2(jnp.minimum(z, MAX_EXP2))))
    if causal is not None:
        sp = jnp.where(causal, sp, 0.0)
    spb = sp.astype(MXU_DTYPE)
    r = _dot(spb, mm)
    cum = r + jnp.concatenate([carry] * (KEY_TILE // LANES), axis=1)
    carry = carry + (r[:, 0:1] + spb[:, 0:1].astype(F32))
    a = jnp.exp2(z - sp - cum)
    if causal is not None:
        a = jnp.where(causal, a, 0.0)
    return _dot(a.astype(MXU_DTYPE), v_tile), carry


def _sb_body(q_ref, kt_ref, v_ref, mm_ref, o_ref, acc_ref, carry_ref, *, tq):
    q0 = pl.program_id(2) * tq
    n_off = q0 // KEY_TILE
    mm = mm_ref[...]
    first = lax.broadcasted_iota(jnp.int32, (1, LANES), 1) < HEAD_DIM
    lanes_of = (first, jnp.logical_not(first))
    qs = [jnp.where(m, q_ref[...], 0) for m in lanes_of]
    acc_ref[...] = jnp.zeros_like(acc_ref)
    carry_ref[...] = jnp.zeros_like(carry_ref)

    def tile(k0, r0, causal):
        kt_tile = kt_ref[:, pl.ds(k0, KEY_TILE)]
        v_tile = v_ref[pl.ds(k0, KEY_TILE), :]
        pvs = []
        for h in range(2):
            pv, carry = _sb_tile(qs[h][r0:], kt_tile, jnp.where(lanes_of[h], v_tile, 0), mm,
                                 carry_ref[h, r0:, :], causal)
            carry_ref[h, r0:, :] = carry
            pvs.append(pv)
        return pvs[0] + pvs[1]

    for d in reversed(range(tq // KEY_TILE)):
        r0 = d * KEY_TILE
        rows = lax.broadcasted_iota(jnp.int32, (tq - r0, KEY_TILE), 0)
        cols = lax.broadcasted_iota(jnp.int32, (tq - r0, KEY_TILE), 1)
        acc_ref[r0:, :] += tile(pl.multiple_of(q0 + r0, KEY_TILE), r0, cols < rows)

    def off_diag(it, _):
        pv = [tile(pl.multiple_of((n_off - 1 - SB_UNROLL * it - u) * KEY_TILE, KEY_TILE), 0, None)
              for u in range(SB_UNROLL)]
        acc_ref[...] += sum(pv[1:], pv[0])
        return 0

    lax.fori_loop(0, n_off // SB_UNROLL, off_diag, 0)
    o_ref[...] = acc_ref[...].astype(o_ref.dtype)


def _sb_attention(qkv, kt, *, tq=512):
    b, s, n3 = qkv.shape
    mix = n3 // 3
    assert s % tq == 0 and tq % (SB_UNROLL * KEY_TILE) == 0 and mix % LANES == 0
    v_col0 = 2 * mix // LANES
    mm = jnp.asarray(_suffix_matrix(), MXU_DTYPE)
    return pl.pallas_call(
        functools.partial(_sb_body, tq=tq),
        grid=(b, mix // LANES, s // tq),
        in_specs=[pl.BlockSpec((None, tq, LANES), lambda bi, hp, i: (bi, i, hp)),
                  pl.BlockSpec((None, LANES, s), lambda bi, hp, i: (bi, hp, 0)),
                  pl.BlockSpec((None, s, LANES), lambda bi, hp, i: (bi, 0, v_col0 + hp)),
                  _const_spec((2 * LANES, 2 * LANES))],
        out_specs=pl.BlockSpec((None, tq, LANES), lambda bi, hp, i: (bi, i, hp)),
        out_shape=jax.ShapeDtypeStruct((b, s, mix), MXU_DTYPE),
        scratch_shapes=[pltpu.VMEM((tq, LANES), F32), pltpu.VMEM((2, tq, LANES), F32)],
        compiler_params=_params(3),
        name="stick_breaking",
    )(qkv, kt, qkv, mm)


def _compress_body(c_ref, pe_ref, w1_ref, w2_ref, o_ref):
    c = c_ref[...]
    nch, half = c.shape
    top = _dot((c + pe_ref[0:1, :]).astype(MXU_DTYPE), w1_ref[:half, :])
    bot = _dot((c + pe_ref[1:2, :]).astype(MXU_DTYPE), w1_ref[half:, :])
    hid = top + pltpu.roll(bot, nch - 1, 0)
    o_ref[...] = _dot(jax.nn.gelu(hid).astype(MXU_DTYPE), w2_ref[...])


def _compress(chunks, pe, w1, w2):
    _, b, hk, nch, half = chunks.shape
    hid = w1.shape[-1]
    return pl.pallas_call(
        _compress_body,
        grid=(2, b, hk),
        in_specs=[pl.BlockSpec((None, None, None, nch, half), lambda c, bi, h: (c, bi, h, 0, 0)),
                  pl.BlockSpec((None, 2, half), lambda c, bi, h: (c, 0, 0)),
                  pl.BlockSpec((None, 2 * half, hid), lambda c, bi, h: (c, 0, 0)),
                  pl.BlockSpec((None, hid, HEAD_DIM), lambda c, bi, h: (c, 0, 0))],
        out_specs=pl.BlockSpec((None, None, None, nch, HEAD_DIM), lambda c, bi, h: (c, bi, h, 0, 0)),
        out_shape=jax.ShapeDtypeStruct((2, b, hk, nch, HEAD_DIM), F32),
        compiler_params=_params(3),
        name="nsa_compress",
    )(chunks, pe, w1, w2)


def _nsa_body(qs_ref, qt_ref, kc_ref, vct_ref, ovt_ref, bct_ref, kst_ref, vs_ref, bs_ref,
              kwt_ref, vw_ref, bw_ref, g_ref, o_ref, sc_ref, *, tq):
    i = pl.program_id(2)
    q0 = i * tq
    rows = GROUP * tq
    ncp = kc_ref.shape[0]
    nsel = ovt_ref.shape[0]
    band = bct_ref.shape[0]

    qt = jnp.concatenate([qt_ref[g] for g in range(GROUP)], axis=1)
    sc_ref[...] = _dot(kc_ref[...], qt)

    @pl.when(i > 0)
    def _():
        j0 = pl.multiple_of(i * (tq // CMP_STRIDE) - CMP_BAND_BACK, CMP_STRIDE)
        sc_ref[pl.ds(j0, band), :] += bct_ref[...]

    @pl.when(i == 0)
    def _():
        sc_ref[:band - CMP_BAND_BACK, :] += bct_ref[CMP_BAND_BACK:, :]

    jp = lax.broadcasted_iota(jnp.int32, (ncp, rows), 0)
    tcol = q0 + (lax.broadcasted_iota(jnp.int32, (ncp, rows), 1) & (tq - 1))
    valid = CMP_STRIDE * jp + (CMP_LEN - 1) <= tcol
    s = jnp.where(valid, sc_ref[...], -jnp.inf)
    m = jnp.max(s, axis=0, keepdims=True)
    e = jnp.exp2(s - jnp.where(m > -jnp.inf, m, 0.0))
    den = jnp.sum(e, axis=0, keepdims=True)
    pc = (e / jnp.where(den > 0, den, 1.0)).astype(MXU_DTYPE)
    o_c = _dot(vct_ref[...], pc).T
    imp = _dot(ovt_ref[...], pc[:, :tq])
    for g in range(1, GROUP):
        imp = imp + _dot(ovt_ref[...], pc[:, g * tq:(g + 1) * tq])

    sid = lax.broadcasted_iota(jnp.int32, (nsel, tq), 0)
    cur = (q0 + lax.broadcasted_iota(jnp.int32, (nsel, tq), 1)) // SEL_LEN
    forced = (sid == 0) | (sid == cur) | (sid == cur - 1)
    v = jnp.where(forced, FORCED_SCORE, jnp.where(sid <= cur, imp, -1.0))
    mask_t = jnp.full((nsel, tq), NEG, F32)
    for _ in range(min(SEL_TOPK, nsel)):
        mx = jnp.max(v, axis=0, keepdims=True)
        idx = jnp.min(jnp.where(v == mx, sid, nsel), axis=0, keepdims=True)
        hit = sid == idx
        mask_t = jnp.where(hit, 0.0, mask_t)
        v = jnp.where(hit, -jnp.inf, v)
    mask = mask_t.T.astype(MXU_DTYPE)
    qs = qs_ref[...].reshape(rows, HEAD_DIM)
    pad_col = jnp.where(lax.broadcasted_iota(jnp.int32, (rows, PAD_ROWS), 1) == 0, NEG, 0.0).astype(MXU_DTYPE)
    qa = jnp.concatenate([jnp.concatenate([mask] * GROUP, axis=0), qs, pad_col], axis=1)

    near = bs_ref.shape[1]

    def scores(k0):
        return _dot(qa, kst_ref[:, pl.ds(pl.multiple_of(k0, KEY_TILE), near)])

    def far_tile(it, carry):
        m_run, acc, s_f = carry
        k0 = far0 + it * near
        s_next = scores(k0 + near)
        m_new = jnp.maximum(m_run, jnp.max(s_f, axis=1, keepdims=True))
        p_f = jnp.exp2(s_f - m_new).astype(MXU_DTYPE)
        acc = jnp.exp2(m_run - m_new) * acc + _dot(p_f, vs_ref[pl.ds(pl.multiple_of(k0, KEY_TILE), near), :])
        return m_new, acc, s_next

    near0 = q0 + WINDOW - (near - tq)
    n_far = (jnp.maximum(q0 - (near - tq), 0) + near - 1) // near
    far0 = near0 - n_far * near
    m_run, acc, s_n = lax.fori_loop(
        0, n_far, far_tile,
        (jnp.full((rows, 1), -jnp.inf, F32), jnp.zeros((rows, LANES), F32), scores(far0)))
    k0 = pl.multiple_of(near0, KEY_TILE)
    s_n = s_n + bs_ref[...]
    m_new = jnp.maximum(m_run, jnp.max(s_n, axis=1, keepdims=True))
    p_n = jnp.exp2(s_n - m_new).astype(MXU_DTYPE)
    acc = jnp.exp2(m_run - m_new) * acc + _dot(p_n, vs_ref[pl.ds(k0, near), :])
    o_s = acc[:, :HEAD_DIM] / acc[:, HEAD_DIM:HEAD_DIM + 1]

    slab = bw_ref.shape[1]
    kw0 = pl.multiple_of(q0, KEY_TILE)
    s_w = _dot(qs, kwt_ref[:, pl.ds(kw0, slab)]) + bw_ref[...]
    col = lax.broadcasted_iota(jnp.int32, (rows, slab), 1)
    s_w = jnp.where(q0 - WINDOW + col >= 0, s_w, NEG)
    p_w = jnp.exp2(s_w - jnp.max(s_w, axis=1, keepdims=True)).astype(MXU_DTYPE)
    acc_w = _dot(p_w, vw_ref[pl.ds(kw0, slab), :])
    o_w = acc_w[:, :HEAD_DIM] / acc_w[:, HEAD_DIM:HEAD_DIM + 1]

    sg = jax.nn.sigmoid(g_ref[...])

    def gate(branch):
        return jnp.concatenate([sg[:, branch * GROUP + g:branch * GROUP + g + 1] for g in range(GROUP)], axis=0)

    out = gate(0) * o_c + gate(1) * o_s + gate(2) * o_w
    o_ref[...] = out.reshape(GROUP, tq, HEAD_DIM).astype(o_ref.dtype)


def _rel_bucket_table(n):
    d = np.arange(n)
    exact = REL_BUCKETS // 2
    nf = np.maximum(d, 1).astype(np.float32)
    large = exact + (np.log(nf / np.float32(exact)) / np.float32(math.log(REL_MAX_DIST / exact))
                     * np.float32(REL_BUCKETS - exact)).astype(np.int32)
    return np.where(d < exact, d, np.minimum(large, REL_BUCKETS - 1)).astype(np.int32)


def _bias_tile(rel_bias, dist, valid, shifted, invalid_value):
    bucket = _rel_bucket_table(int(dist.max()) + 1)[np.maximum(dist, 0)]
    tile = rel_bias[bucket]
    if shifted:
        tile = tile - rel_bias[REL_BUCKETS - 1]
    tile = tile * LOG2E
    tile = jnp.where(valid[..., None], tile, invalid_value)
    tile = jnp.moveaxis(tile, -1, 0)
    return tile.reshape((N_KV_HEADS, GROUP) + dist.shape)


def _toeplitz_bias(rel_bias, n_rows, n_cols, off, valid, shifted, invalid_value):
    period = n_rows + n_cols
    x = np.arange(period)
    x = np.where(x < n_cols, x, x - period)
    dist = np.maximum(off - x, 0)
    u = rel_bias[_rel_bucket_table(int(dist.max()) + 1)[dist]]
    if shifted:
        u = u - rel_bias[REL_BUCKETS - 1]
    u = u * LOG2E
    flat = jnp.tile(u.T, (1, n_rows))[:, :n_rows * (period - 1)]
    tile = flat.reshape(-1, n_rows, period - 1)[:, :, :n_cols]
    tile = jnp.where(valid[None], tile, invalid_value)
    return tile.reshape(N_KV_HEADS, GROUP * n_rows, n_cols)


def _nsa_attention(qs, qt, kc, vct, kst, vs, kwt, vw, gates, rel_bias, *, tq=256):
    b, nh, s, _ = qs.shape
    ncp = kc.shape[2]
    nsel = s // SEL_LEN
    sp = kst.shape[-1]
    rows = GROUP * tq
    near = 2 * tq
    slab = tq + WINDOW
    band = tq // CMP_STRIDE + CMP_BAND_BACK
    assert s % tq == 0 and tq % KEY_TILE == 0 and tq & (tq - 1) == 0 and sp == s + WINDOW

    jc = np.arange(ncp)[None, :]
    cs = jc * CMP_STRIDE
    ss = np.arange(nsel)[:, None] * SEL_LEN
    ovt = (jc < (s - CMP_LEN) // CMP_STRIDE + 1) & (cs < ss + SEL_LEN) & (cs + CMP_LEN > ss)
    ovt = jnp.asarray(ovt.astype(np.float32), MXU_DTYPE)

    r = np.arange(tq)
    d_c = r[None, :] + (CMP_BAND_BACK * CMP_STRIDE - (CMP_LEN - 1)) - CMP_STRIDE * np.arange(band)[:, None]
    bct = _bias_tile(rel_bias, d_c, d_c >= 0, True, 0.0)
    bct = bct.transpose(0, 2, 1, 3).reshape(N_KV_HEADS, band, rows)
    d_s = r[:, None] - np.arange(near)[None, :] + (near - tq)
    bs = _toeplitz_bias(rel_bias, tq, near, near - tq, d_s >= 0, True, NEG)
    d_w = r[:, None] - np.arange(slab)[None, :] + WINDOW
    bw = _toeplitz_bias(rel_bias, tq, slab, WINDOW, (d_w >= 0) & (d_w < WINDOW), False, NEG)

    per_head = lambda *shape: pl.BlockSpec((None, None) + shape, lambda bi, h, i: (bi, h, 0, 0))
    per_kvh = lambda *shape: pl.BlockSpec((None,) + shape, lambda bi, h, i: (h, 0, 0))
    return pl.pallas_call(
        functools.partial(_nsa_body, tq=tq),
        grid=(b, N_KV_HEADS, s // tq),
        in_specs=[pl.BlockSpec((None, GROUP, tq, HEAD_DIM), lambda bi, h, i: (bi, h, i, 0)),
                  pl.BlockSpec((None, GROUP, HEAD_DIM, tq), lambda bi, h, i: (bi, h, 0, i)),
                  per_head(ncp, HEAD_DIM),
                  per_head(HEAD_DIM, ncp),
                  _const_spec((nsel, ncp)),
                  per_kvh(band, rows),
                  per_head(nsel + HEAD_DIM + PAD_ROWS, sp),
                  per_head(sp, LANES),
                  per_kvh(rows, near),
                  per_head(HEAD_DIM, sp),
                  per_head(sp, LANES),
                  per_kvh(rows, slab),
                  pl.BlockSpec((None, None, tq, 16), lambda bi, h, i: (bi, h, i, 0))],
        out_specs=pl.BlockSpec((None, GROUP, tq, HEAD_DIM), lambda bi, h, i: (bi, h, i, 0)),
        out_shape=jax.ShapeDtypeStruct((b, nh, s, HEAD_DIM), MXU_DTYPE),
        scratch_shapes=[pltpu.VMEM((ncp, rows), F32)],
        compiler_params=_params(3),
        name="nsa_attention",
    )(qs, qt, kc, vct, ovt, bct, kst, vs, bs, kwt, vw, bw, gates)


def _with_ones(v):
    ones = jnp.ones(v.shape[:-1] + (1,), v.dtype)
    zeros = jnp.zeros(v.shape[:-1] + (LANES - HEAD_DIM - 1,), v.dtype)
    return jnp.concatenate([v, ones, zeros], axis=-1)


def _pad_keys(x, axis):
    pad = [(0, 0)] * x.ndim
    pad[axis] = (WINDOW, 0)
    return jnp.pad(x, pad)


def _nsa_mixer(hn, w_in, rel_bias, pe_k, pe_v, ck_w1, ck_w2, cv_w1, cv_w2, norm_g, b, s):
    d = hn.shape[-1]
    mix = N_HEADS * HEAD_DIM
    kvw = N_KV_HEADS * HEAD_DIM
    kv_end = mix + 6 * kvw
    n_gate = 3 * N_HEADS
    hi_pad = -(2 * kvw + n_gate) % KEY_TILE
    w = jnp.concatenate([w_in[:, :mix], w_in[:, mix + 2 * kvw:kv_end], w_in[:, mix:mix + 2 * kvw],
                         w_in[:, kv_end:], jnp.zeros((d, hi_pad), w_in.dtype)], axis=1).astype(MXU_DTYPE)
    n_lo = mix + 4 * kvw
    n_hi = 2 * kvw + n_gate + hi_pad
    col_scale = jnp.where(np.arange(n_lo + n_hi) < mix, Q_SCALE, 1.0).astype(F32)
    lo, hi = _norm_matmul(hn, norm_g, w, col_scale, ((n_lo, MXU_DTYPE), (n_hi, F32)), col_chunk=KEY_TILE)
    lo = lo.reshape(b, s, n_lo)
    hi = hi.reshape(b, s, n_hi)

    q = lo[..., :mix].reshape(b, s, N_HEADS, HEAD_DIM)
    qs = q.transpose(0, 2, 1, 3)
    qt = q.transpose(0, 2, 3, 1)
    kv = lo[..., mix:].reshape(b, s, 4, N_KV_HEADS, HEAD_DIM)
    k_sel_t, k_win_t = (kv[:, :, part].transpose(0, 2, 3, 1) for part in (0, 2))
    v_sel, v_win = (kv[:, :, part].transpose(0, 2, 1, 3) for part in (1, 3))
    kv_cmp = hi[..., :2 * kvw].reshape(b, s, 2, N_KV_HEADS, HEAD_DIM).transpose(2, 0, 3, 1, 4)
    gates = hi[..., 2 * kvw:2 * kvw + n_gate].reshape(b, s, 3, N_KV_HEADS, GROUP)
    gates = gates.transpose(0, 3, 1, 2, 4).reshape(b, N_KV_HEADS, s, 3 * GROUP)
    gates = jnp.pad(gates, ((0, 0), (0, 0), (0, 0), (0, 16 - 3 * GROUP)))

    nch = s // CMP_STRIDE
    half = CMP_STRIDE * HEAD_DIM
    chunks = kv_cmp.reshape(2, b, N_KV_HEADS, nch, half)
    pe = jnp.stack([pe_k, pe_v]).reshape(2, 2, half)
    w1 = jnp.stack([ck_w1, cv_w1]).astype(MXU_DTYPE)
    w2 = jnp.stack([ck_w2, cv_w2]).astype(MXU_DTYPE)
    cmp = _compress(chunks, pe, w1, w2)
    cmp = cmp.astype(MXU_DTYPE)
    kc = cmp[0]
    vct = cmp[1].transpose(0, 1, 3, 2)

    nsel = s // SEL_LEN
    onehot = (np.arange(s)[None, :] // SEL_LEN == np.arange(nsel)[:, None]).astype(np.float32)
    onehot = jnp.broadcast_to(jnp.asarray(onehot, MXU_DTYPE), (b, N_KV_HEADS, nsel, s))
    kst = _pad_keys(jnp.concatenate([onehot, k_sel_t], axis=2), 3)
    is_pad = (np.arange(s + WINDOW)[None, :] < WINDOW) & (np.arange(PAD_ROWS)[:, None] == 0)
    is_pad = jnp.broadcast_to(jnp.asarray(is_pad.astype(np.float32), MXU_DTYPE), kst.shape[:2] + is_pad.shape)
    kst = jnp.concatenate([kst, is_pad], axis=2)
    vs = _pad_keys(_with_ones(v_sel), 2)
    kwt = _pad_keys(k_win_t, 3)
    vw = _pad_keys(_with_ones(v_win), 2)

    out = _nsa_attention(qs, qt, kc, vct, kst, vs, kwt, vw, gates, rel_bias)
    return out.transpose(0, 2, 1, 3).reshape(b * s, mix)


def _sb_mixer(hn, w_in, norm_g, b, s):
    mix = N_HEADS * HEAD_DIM
    col_scale = jnp.where(np.arange(3 * mix) < mix, Q_SCALE, 1.0).astype(F32)
    qkv, = _norm_matmul(hn, norm_g, w_in.astype(MXU_DTYPE), col_scale, ((3 * mix, MXU_DTYPE),))
    qkv = qkv.reshape(b, s, 3 * mix)
    kt = qkv[..., mix:2 * mix].transpose(0, 2, 1)
    return _sb_attention(qkv, kt).reshape(b * s, mix)


def kernel(x, p, rel_bias, norm_mix, norm_ffn, norm_ple, final_norm, sb_w_in, sb_w_out, nsa_w_in, nsa_w_out,
           nsa_pe_k, nsa_pe_v, nsa_ck_w1, nsa_ck_w2, nsa_cv_w1, nsa_cv_w2, ffn_w_in, ffn_w_out,
           ple_w_proj, ple_w_gate):
    b, s, d = x.shape
    depth = p.shape[0]
    h = x.reshape(b * s, d)
    for i in range(depth):
        j = i // 2
        if i % 2 == 0:
            attn = _sb_mixer(h, sb_w_in[j], norm_mix[i], b, s)
            wo = sb_w_out[j]
        else:
            attn = _nsa_mixer(h, nsa_w_in[j], rel_bias, nsa_pe_k[j], nsa_pe_v[j], nsa_ck_w1[j], nsa_ck_w2[j],
                              nsa_cv_w1[j], nsa_cv_w2[j], norm_mix[i], b, s)
            wo = nsa_w_out[j]
        h = _post_mixer(h, attn, wo.astype(MXU_DTYPE), norm_ffn[i], ffn_w_in[i].astype(MXU_DTYPE),
                        ffn_w_out[i].astype(MXU_DTYPE), norm_ple[i], ple_w_gate[i].astype(MXU_DTYPE),
                        p[i].reshape(b * s, -1), ple_w_proj[i].astype(MXU_DTYPE), final_norm,
                        final=(i == depth - 1))
    return h.reshape(b, s, d)
```
